```python
import jax, jax.numpy as jnp
from jax import lax
import numpy as np

D_MODEL = 1024
BATCH = 8
SEQ = 2048
DEPTH = 2

PLE_DIM = 256
EPS = 1e-6
A_HEAD_DIM = 128
A_HEADS = (D_MODEL // 2) // A_HEAD_DIM
A_DIM = A_HEADS * A_HEAD_DIM
QKV_CONV_WIDTH = 4
CHUNK = 64
POOL_WINDOWS = (2, 4, 8, 16)
POOL_GROUPS = len(POOL_WINDOWS)
POOL_DIM = D_MODEL // 4
POOL_GROUP_DIM = POOL_DIM // POOL_GROUPS
CONV_HEADS = 4
CONV_DIM = D_MODEL // 4
CONV_WIDTH = 3
D_MIX = A_DIM + POOL_DIM + CONV_DIM
IN_SIZES = (A_DIM, A_DIM, A_DIM, A_DIM, A_HEADS, A_HEADS, POOL_DIM, CONV_DIM, CONV_DIM, CONV_DIM)
D_IN = sum(IN_SIZES)
D_FF = -(-8 * D_MODEL // (3 * 256)) * 256

kernel_name = 'hybrid_parallel_deltanet_pool_shortconv'


def rms_norm(x, g):
    xf = x.astype(jnp.float32)
    y = xf * lax.rsqrt(jnp.mean(xf * xf, axis=-1, keepdims=True) + EPS)
    return (y * g.astype(jnp.float32)).astype(x.dtype)


def causal_dwconv(x, w):
    K, C = w.shape
    return lax.conv_general_dilated(
        x, w[:, None, :].astype(x.dtype), window_strides=(1,), padding=[(K - 1, 0)],
        dimension_numbers=('NWC', 'WIO', 'NWC'), feature_group_count=C)


def l2norm(t):
    return t * lax.rsqrt(jnp.sum(t * t, axis=-1, keepdims=True) + EPS)


def chunk_gated_delta_rule(q, k, v, g, beta):
    Bn, S, H, DK = q.shape
    DV = v.shape[-1]
    N = S // CHUNK

    def to_chunks(t):
        t = t.reshape((Bn, N, CHUNK, H) + t.shape[3:])
        return jnp.moveaxis(t, 3, 1)

    q = to_chunks(q * (DK ** -0.5))
    k = to_chunks(k)
    v = to_chunks(v)
    g = to_chunks(g)
    beta = to_chunks(beta)
    gc = jnp.cumsum(g, axis=-1)
    kb = k * beta[..., None]
    vb = v * beta[..., None]
    causal_incl = jnp.tril(jnp.ones((CHUNK, CHUNK), dtype=bool))
    causal_strict = jnp.tril(jnp.ones((CHUNK, CHUNK), dtype=bool), -1)
    diff = gc[..., :, None] - gc[..., None, :]
    decay = jnp.exp(jnp.where(causal_incl, diff, -jnp.inf))
    lower = jnp.where(causal_strict, jnp.einsum('bhncd,bhnsd->bhncs', kb, k) * decay, 0.0)
    eye = jnp.eye(CHUNK, dtype=jnp.float32)
    tmat = lax.linalg.triangular_solve(eye + lower, jnp.broadcast_to(eye, lower.shape),
                                       left_side=True, lower=True, unit_diagonal=True)
    u = jnp.einsum('bhncs,bhnsv->bhncv', tmat, vb)
    w = jnp.einsum('bhncs,bhnsd->bhncd', tmat, kb * jnp.exp(gc)[..., None])
    attn = jnp.einsum('bhncd,bhnsd->bhncs', q, k) * decay

    def step(state, inp):
        q_i, k_i, u_i, w_i, gc_i, a_i = inp
        v_new = u_i - jnp.einsum('bhck,bhkv->bhcv', w_i, state)
        o_i = (jnp.einsum('bhck,bhkv->bhcv', q_i * jnp.exp(gc_i)[..., None], state)
               + jnp.einsum('bhcs,bhsv->bhcv', a_i, v_new))
        g_last = gc_i[..., -1]
        state = (state * jnp.exp(g_last)[..., None, None]
                 + jnp.einsum('bhck,bhcv->bhkv', k_i * jnp.exp(g_last[..., None] - gc_i)[..., None], v_new))
        return state, o_i

    xs = tuple(jnp.moveaxis(t, 2, 0) for t in (q, k, u, w, gc, attn))
    state0 = jnp.zeros((Bn, H, DK, DV), jnp.float32)
    _, o = lax.scan(step, state0, xs)
    return jnp.transpose(o, (1, 0, 3, 2, 4)).reshape(Bn, S, H, DV)


def gated_deltanet(q, k, v, z, a, b, conv_w, a_log, dt_bias, onorm_g):
    Bn, S, _ = q.shape
    qkv = jax.nn.silu(causal_dwconv(jnp.concatenate([q, k, v], axis=-1), conv_w))
    q, k, v = jnp.split(qkv.astype(jnp.float32), 3, axis=-1)
    hs = (Bn, S, A_HEADS, A_HEAD_DIM)
    q = l2norm(q.reshape(hs))
    k = l2norm(k.reshape(hs))
    v = v.reshape(hs)
    beta = jax.nn.sigmoid(b.astype(jnp.float32))
    g = -jnp.exp(a_log.astype(jnp.float32)) * jax.nn.softplus(a.astype(jnp.float32) + dt_bias.astype(jnp.float32))
    o = chunk_gated_delta_rule(q, k, v, g, beta)
    o = o * lax.rsqrt(jnp.mean(o * o, axis=-1, keepdims=True) + EPS) * onorm_g.astype(jnp.float32)
    o = o * jax.nn.silu(z.astype(jnp.float32).reshape(hs))
    return o.reshape(Bn, S, A_DIM).astype(z.dtype)


def multiscale_pool(h, pool_w, pool_scale):
    Bn, S, _ = h.shape
    hf = h.astype(jnp.float32)
    cs = jnp.cumsum(hf, axis=1)
    count = jnp.arange(1, S + 1, dtype=jnp.float32)[:, None]
    outs = []
    for gi, win in enumerate(POOL_WINDOWS):
        sl = slice(gi * POOL_GROUP_DIM, (gi + 1) * POOL_GROUP_DIM)
        csg = cs[..., sl]
        lag = jnp.pad(csg, ((0, 0), (win, 0), (0, 0)))[:, :S]
        mean = (csg - lag) / jnp.minimum(count, float(win))
        outs.append(mean - hf[..., sl])
    pooled = jnp.stack(outs, axis=2)
    y = jnp.einsum('bsgc,gcd->bsgd', pooled, pool_w.astype(jnp.float32)).reshape(Bn, S, POOL_DIM)
    return (y * pool_scale.astype(jnp.float32)).astype(h.dtype)


def short_gated_conv(gate_b, gate_c, hc, conv_w):
    return gate_b * causal_dwconv(gate_c * hc, conv_w)


def setup_inputs(seed: int = 0) -> dict:
    key = jax.random.key(seed)
    ks = jax.random.split(key, 24)
    f32 = jnp.float32
    nrm = lambda k, shape, scale: jax.random.normal(k, shape, f32) * scale
    dt = jnp.exp(jax.random.uniform(ks[5], (DEPTH, A_HEADS), f32, np.log(1e-3), np.log(1e-1)))
    return {
        'x': nrm(ks[0], (BATCH, SEQ, D_MODEL), 1.0),
        'p': nrm(ks[1], (DEPTH, BATCH, SEQ, PLE_DIM), 1.0),
        'norm1_g': 1.0 + nrm(ks[2], (DEPTH, D_MODEL), 0.02),
        'w_in': nrm(ks[3], (DEPTH, D_MODEL, D_IN), D_MODEL ** -0.5),
        'conv_qkv': nrm(ks[4], (DEPTH, QKV_CONV_WIDTH, 3 * A_DIM), QKV_CONV_WIDTH ** -0.5),
        'a_log': jnp.log(jax.random.uniform(ks[6], (DEPTH, A_HEADS), f32, 1.0, 16.0)),
        'dt_bias': jnp.log(jnp.expm1(dt)),
        'onorm_g': 1.0 + nrm(ks[7], (DEPTH, A_HEAD_DIM), 0.02),
        'pool_w': nrm(ks[8], (DEPTH, POOL_GROUPS, POOL_GROUP_DIM, POOL_GROUP_DIM), POOL_GROUP_DIM ** -0.5),
        'pool_scale': 1.0 + nrm(ks[9], (DEPTH, POOL_DIM), 0.02),
        'sconv_w': nrm(ks[10], (DEPTH, CONV_WIDTH, CONV_DIM), CONV_WIDTH ** -0.5),
        'w_out': nrm(ks[11], (DEPTH, D_MIX, D_MODEL), D_MIX ** -0.5),
        'norm2_g': 1.0 + nrm(ks[12], (DEPTH, D_MODEL), 0.02),
        'w_gate': nrm(ks[13], (DEPTH, D_MODEL, D_FF), D_MODEL ** -0.5),
        'w_up': nrm(ks[14], (DEPTH, D_MODEL, D_FF), D_MODEL ** -0.5),
        'w_down': nrm(ks[15], (DEPTH, D_FF, D_MODEL), D_FF ** -0.5),
        'ple_proj': nrm(ks[16], (DEPTH, PLE_DIM, D_MODEL), PLE_DIM ** -0.5),
        'ple_gate': nrm(ks[17], (DEPTH, D_MODEL, D_MODEL), D_MODEL ** -0.5),
        'final_g': 1.0 + nrm(ks[18], (D_MODEL,), 0.02),
    }


def reference(x, p, norm1_g, w_in, conv_qkv, a_log, dt_bias, onorm_g, pool_w, pool_scale,
              sconv_w, w_out, norm2_g, w_gate, w_up, w_down, ple_proj, ple_gate, final_g):
    offsets = [0]
    for s in IN_SIZES[:-1]:
        offsets.append(offsets[-1] + s)
    for i in range(DEPTH):
        h = rms_norm(x, norm1_g[i])
        proj = jnp.einsum('bsd,de->bse', h, w_in[i])
        q, k, v, z, a, b, hp, cb, cc, ch = jnp.split(proj, offsets[1:], axis=-1)
        o_a = gated_deltanet(q, k, v, z, a, b, conv_qkv[i], a_log[i], dt_bias[i], onorm_g[i])
        o_b = multiscale_pool(hp, pool_w[i], pool_scale[i])
        o_c = short_gated_conv(cb, cc, ch, sconv_w[i])
        mixed = jnp.concatenate([o_a, o_b, o_c], axis=-1)
        x = x + jnp.einsum('bse,ed->bsd', mixed, w_out[i])
        h = rms_norm(x, norm2_g[i])
        ff = jax.nn.silu(jnp.einsum('bsd,df->bsf', h, w_gate[i])) * jnp.einsum('bsd,df->bsf', h, w_up[i])
        x = x + jnp.einsum('bsf,fd->bsd', ff, w_down[i])
        gate = jax.nn.sigmoid(jnp.einsum('bsd,de->bse', x, ple_gate[i]).astype(jnp.float32)).astype(x.dtype)
        x = x + gate * jnp.einsum('bsq,qd->bsd', p[i], ple_proj[i])
    return rms_norm(x, final_g)
```

```python
import functools

import jax
import jax.numpy as jnp
from jax import lax
from jax.experimental import pallas as pl
from jax.experimental.pallas import tpu as pltpu

D_MODEL = 1024
PLE_DIM = 256
EPS = 1e-6
HEAD_DIM = 128
HEADS = 4
A_DIM = HEADS * HEAD_DIM
QKV_DIM = 3 * A_DIM
QKV_CONV_WIDTH = 4
CHUNK = 64
POOL_WINDOWS = (2, 4, 8, 16)
POOL_DIM = 256
POOL_GROUP_DIM = 64
CONV_DIM = 256
CONV_WIDTH = 3
D_FF = 2816
AB_PAD = 128
REST_DIM = A_DIM + POOL_DIM + 3 * CONV_DIM + AB_PAD
Z_OFF, POOL_OFF, CB_OFF, CC_OFF, CH_OFF, AB_OFF = 0, 512, 768, 1024, 1280, 1536
D_IN_PAD = QKV_DIM + REST_DIM

QKV_HIST = 8
POOL_HIST = 16
VMEM_LIMIT_BYTES = 56 * 1024 * 1024
FFN_TILE = 512
FF_CHUNK = 256

BF16 = jnp.bfloat16
F32 = jnp.float32


def _dot(a, b):
    return jnp.dot(a.astype(BF16), b.astype(BF16), preferred_element_type=F32)


def _dot_nt(a, b):
    return lax.dot_general(a.astype(BF16), b.astype(BF16), (((1,), (1,)), ((), ())),
                           preferred_element_type=F32)


def _sigmoid(x):
    return 1.0 / (1.0 + jnp.exp(-x))


def _silu(x):
    return x * _sigmoid(x)


def _softplus(x):
    return jnp.maximum(x, 0.0) + jnp.log1p(jnp.exp(-jnp.abs(x)))


def _rms_norm(x, g):
    return x * lax.rsqrt(jnp.mean(x * x, axis=-1, keepdims=True) + EPS) * g


def _chunk_head(qr, kr, v, gcol, grow, beta, state):
    c = CHUNK
    rq = lax.rsqrt(jnp.sum(qr * qr, axis=-1, keepdims=True) + EPS) * (HEAD_DIM ** -0.5)
    rk = lax.rsqrt(jnp.sum(kr * kr, axis=-1, keepdims=True) + EPS)
    q = qr * rq
    k = kr * rk
    eg = jnp.exp(gcol)
    g_last = gcol[c - 1:c, :]
    kb = k * beta
    vb = v * beta
    row = lax.broadcasted_iota(jnp.int32, (c, c), 0)
    col = lax.broadcasted_iota(jnp.int32, (c, c), 1)
    decay = jnp.exp(jnp.where(row >= col, gcol - grow, -1e30))
    kk = _dot_nt(kb, k)
    qk = _dot_nt(q, k)
    lower = jnp.where(row > col, kk * decay, 0.0)
    attn = qk * decay
    x = -lower
    t = jnp.where(row == col, 1.0, 0.0) + x
    n_sq = CHUNK.bit_length() - 2
    for _ in range(n_sq):
        x = _dot(x, x)
        t = t + _dot(t, x)
    u = _dot(t, vb)
    w = _dot(t, kb * eg)
    v_new = u - _dot(w, state)
    o = _dot(q * eg, state) + _dot(attn, v_new)
    kd = k * jnp.exp(g_last - gcol)
    new_state = state * jnp.exp(g_last) + _dot(kd.T, v_new)
    return o, new_state


def _mixer_kernel(x_ref, n1_ref, win_ref, cw_ref, alog_ref, dtb_ref, on_ref, pw_ref, ps_ref,
                  sw_ref, wout_ref, o_ref,
                  qkv_ext, rest_buf, pool_ext, m_ext, mixed_buf, state_ref):
    s = pl.program_id(0)
    nb = x_ref.shape[0]
    c = CHUNK

    @pl.when(s == 0)
    def _():
        qkv_ext[:, 0:QKV_HIST, :] = jnp.zeros((nb, QKV_HIST, QKV_DIM), F32)
        pool_ext[:, 0:POOL_HIST, :] = jnp.zeros((nb, POOL_HIST, POOL_DIM), F32)
        m_ext[:, 0:QKV_HIST, :] = jnp.zeros((nb, QKV_HIST, CONV_DIM), F32)
        state_ref[...] = jnp.zeros(state_ref.shape, F32)

    x = x_ref[...].reshape(nb * c, D_MODEL)
    h = _rms_norm(x, n1_ref[...]).astype(BF16)
    qkv_ext[:, QKV_HIST:QKV_HIST + c, :] = jnp.dot(
        h, win_ref[:, 0:QKV_DIM], preferred_element_type=F32).reshape(nb, c, QKV_DIM)
    rest_buf[...] = jnp.dot(
        h, win_ref[:, QKV_DIM:D_IN_PAD], preferred_element_type=F32).reshape(nb, c, REST_DIM)

    row_c = lax.broadcasted_iota(jnp.int32, (c, AB_PAD), 0)
    pos = (s * c + 1 + lax.broadcasted_iota(jnp.int32, (c, POOL_DIM), 0)).astype(F32)
    lane_p = lax.broadcasted_iota(jnp.int32, (c, POOL_DIM), 1)
    win = jnp.where(lane_p < 64, 2.0, jnp.where(lane_p < 128, 4.0, jnp.where(lane_p < 192, 8.0, 16.0)))
    pool_den = jnp.minimum(pos, win)

    def per_batch(b, carry):
        ext = qkv_ext[b]
        acc = ext * cw_ref[QKV_CONV_WIDTH - 1:QKV_CONV_WIDTH, :]
        for j in range(QKV_CONV_WIDTH - 1):
            shift = QKV_CONV_WIDTH - 1 - j
            acc = acc + pltpu.roll(ext, shift, 0) * cw_ref[j:j + 1, :]
        act = _silu(acc[QKV_HIST:QKV_HIST + c, :])
        qkv_ext[b, 0:QKV_HIST, :] = ext[c:c + QKV_HIST, :]

        ab = rest_buf[b, :, AB_OFF:AB_OFF + AB_PAD]
        g = -jnp.exp(alog_ref[...]) * _softplus(ab + dtb_ref[...])
        beta_all = _sigmoid(ab)
        gc = g
        sh = 1
        while sh < c:
            gc = gc + jnp.where(row_c >= sh, pltpu.roll(gc, sh, 0), 0.0)
            sh *= 2
        gct = gc.T

        for hd in range(HEADS):
            lo = hd * HEAD_DIM
            o, new_state = _chunk_head(
                act[:, lo:lo + HEAD_DIM],
                act[:, A_DIM + lo:A_DIM + lo + HEAD_DIM],
                act[:, 2 * A_DIM + lo:2 * A_DIM + lo + HEAD_DIM],
                gc[:, hd:hd + 1], gct[hd:hd + 1, :], beta_all[:, HEADS + hd:HEADS + hd + 1],
                state_ref[b, hd])
            state_ref[b, hd] = new_state
            z = rest_buf[b, :, Z_OFF + lo:Z_OFF + lo + HEAD_DIM]
            o = o * lax.rsqrt(jnp.mean(o * o, axis=-1, keepdims=True) + EPS) * on_ref[...]
            mixed_buf[b, :, lo:lo + HEAD_DIM] = o * _silu(z)

        hp = rest_buf[b, :, POOL_OFF:POOL_OFF + POOL_DIM]
        pool_ext[b, POOL_HIST:POOL_HIST + c, :] = hp
        pext = pool_ext[b]
        s2 = pext + pltpu.roll(pext, 1, 0)
        s4 = s2 + pltpu.roll(s2, 2, 0)
        s8 = s4 + pltpu.roll(s4, 4, 0)
        s16 = s8 + pltpu.roll(s8, 8, 0)
        sl = slice(POOL_HIST, POOL_HIST + c)
        sums = jnp.where(lane_p < 64, s2[sl], jnp.where(lane_p < 128, s4[sl],
                         jnp.where(lane_p < 192, s8[sl], s16[sl])))
        pooled = sums / pool_den - hp
        y = jnp.dot(pooled.astype(BF16), pw_ref[...], preferred_element_type=F32) * ps_ref[...]
        mixed_buf[b, :, A_DIM:A_DIM + POOL_DIM] = y
        pool_ext[b, 0:POOL_HIST, :] = pext[c:c + POOL_HIST, :]

        m = rest_buf[b, :, CC_OFF:CC_OFF + CONV_DIM] * rest_buf[b, :, CH_OFF:CH_OFF + CONV_DIM]
        m_ext[b, QKV_HIST:QKV_HIST + c, :] = m
        mext = m_ext[b]
        yc = (mext * sw_ref[2:3, :] + pltpu.roll(mext, 1, 0) * sw_ref[1:2, :]
              + pltpu.roll(mext, 2, 0) * sw_ref[0:1, :])
        cb = rest_buf[b, :, CB_OFF:CB_OFF + CONV_DIM]
        mixed_buf[b, :, A_DIM + POOL_DIM:D_MODEL] = cb * yc[QKV_HIST:QKV_HIST + c, :]
        m_ext[b, 0:QKV_HIST, :] = mext[c:c + QKV_HIST, :]
        return carry

    lax.fori_loop(0, nb, per_batch, 0)

    mixed = mixed_buf[...].reshape(nb * c, D_MODEL).astype(BF16)
    out = x + jnp.dot(mixed, wout_ref[...], preferred_element_type=F32)
    o_ref[...] = out.reshape(nb, c, D_MODEL)


def _ffn_kernel(x_ref, p_ref, n2_ref, wg_ref, wu_ref, wd_ref, pg_ref, pp_ref, fg_ref, o_ref, ff_buf,
                *, final_norm):
    x = x_ref[...]
    h = _rms_norm(x, n2_ref[...]).astype(BF16)
    for ci in range(D_FF // FF_CHUNK):
        sl = slice(ci * FF_CHUNK, (ci + 1) * FF_CHUNK)
        gate = jnp.dot(h, wg_ref[:, sl], preferred_element_type=F32)
        up = jnp.dot(h, wu_ref[:, sl], preferred_element_type=F32)
        ff_buf[:, sl] = (_silu(gate) * up).astype(BF16)
    x = x + jnp.dot(ff_buf[...], wd_ref[...], preferred_element_type=F32)
    gate = _sigmoid(jnp.dot(x.astype(BF16), pg_ref[...], preferred_element_type=F32))
    emb = jnp.dot(p_ref[...].astype(BF16), pp_ref[...], preferred_element_type=F32)
    x = x + gate * emb
    if final_norm:
        x = _rms_norm(x, fg_ref[...])
    o_ref[...] = x


def _const_spec(shape):
    nd = len(shape)
    return pl.BlockSpec(shape, lambda *_: (0,) * nd, pipeline_mode=pl.Buffered(1))


def _mixer_call(x, n1, win, cw, alog, dtb, on, pw, ps, sw, wout):
    nb, seq, _ = x.shape
    c = CHUNK
    tile = pl.BlockSpec((nb, c, D_MODEL), lambda s: (0, s, 0))
    consts = (n1, win, cw, alog, dtb, on, pw, ps, sw, wout)
    return pl.pallas_call(
        _mixer_kernel,
        grid=(seq // c,),
        in_specs=[tile] + [_const_spec(a.shape) for a in consts],
        out_specs=tile,
        out_shape=jax.ShapeDtypeStruct(x.shape, F32),
        scratch_shapes=[
            pltpu.VMEM((nb, QKV_HIST + c, QKV_DIM), F32),
            pltpu.VMEM((nb, c, REST_DIM), F32),
            pltpu.VMEM((nb, POOL_HIST + c, POOL_DIM), F32),
            pltpu.VMEM((nb, QKV_HIST + c, CONV_DIM), F32),
            pltpu.VMEM((nb, c, D_MODEL), F32),
            pltpu.VMEM((nb, HEADS, HEAD_DIM, HEAD_DIM), F32),
        ],
        compiler_params=pltpu.CompilerParams(
            dimension_semantics=("arbitrary",), vmem_limit_bytes=VMEM_LIMIT_BYTES),
        name="mixer",
    )(x, *consts)


def _ffn_call(x2d, p2d, n2, wg, wu, wd, pg, pp, fg, final_norm):
    tokens = x2d.shape[0]
    consts = (n2, wg, wu, wd, pg, pp, fg)
    return pl.pallas_call(
        functools.partial(_ffn_kernel, final_norm=final_norm),
        grid=(tokens // FFN_TILE,),
        in_specs=[pl.BlockSpec((FFN_TILE, D_MODEL), lambda i: (i, 0)),
                  pl.BlockSpec((FFN_TILE, PLE_DIM), lambda i: (i, 0))]
                 + [_const_spec(a.shape) for a in consts],
        out_specs=pl.BlockSpec((FFN_TILE, D_MODEL), lambda i: (i, 0)),
        out_shape=jax.ShapeDtypeStruct(x2d.shape, F32),
        scratch_shapes=[pltpu.VMEM((FFN_TILE, D_FF), BF16)],
        compiler_params=pltpu.CompilerParams(
            dimension_semantics=("arbitrary",), vmem_limit_bytes=VMEM_LIMIT_BYTES),
        name="ffn",
    )(x2d, p2d, *consts)


def _lane_row(vals, offset):
    return jnp.zeros((1, AB_PAD), F32).at[0, offset:offset + vals.shape[0]].set(vals.astype(F32))


def kernel(x, p, norm1_g, w_in, conv_qkv, a_log, dt_bias, onorm_g, pool_w, pool_scale, sconv_w, w_out,
           norm2_g, w_gate, w_up, w_down, ple_proj, ple_gate, final_g):
    depth = w_in.shape[0]
    nb, seq, _ = x.shape
    ab_lo = 4 * A_DIM
    rest_lo = ab_lo + 2 * HEADS
    for i in range(depth):
        wi = w_in[i]
        win = jnp.concatenate(
            [wi[:, :ab_lo], wi[:, rest_lo:], wi[:, ab_lo:rest_lo],
             jnp.zeros((D_MODEL, AB_PAD - 2 * HEADS), wi.dtype)], axis=1).astype(BF16)
        pw = jax.scipy.linalg.block_diag(*[pool_w[i, g] for g in range(len(POOL_WINDOWS))]).astype(BF16)
        x = _mixer_call(
            x, norm1_g[i][None, :], win, conv_qkv[i], _lane_row(a_log[i], 0), _lane_row(dt_bias[i], 0),
            onorm_g[i][None, :], pw, pool_scale[i][None, :], sconv_w[i], w_out[i].astype(BF16))
        x = _ffn_call(
            x.reshape(nb * seq, D_MODEL), p[i].reshape(nb * seq, PLE_DIM), norm2_g[i][None, :],
            w_gate[i].astype(BF16), w_up[i].astype(BF16), w_down[i].astype(BF16),
            ple_gate[i].astype(BF16), ple_proj[i].astype(BF16), final_g[None, :],
            final_norm=(i == depth - 1)).reshape(nb, seq, D_MODEL)
    return x
```

```python
import functools

import jax
import jax.numpy as jnp
from jax import lax
from jax.experimental import pallas as pl
from jax.experimental.pallas import tpu as pltpu

D_MODEL = 1024
PLE_DIM = 256
EPS = 1e-6
HEAD_DIM = 128
HEADS = 4
A_DIM = HEADS * HEAD_DIM
QKV_DIM = 3 * A_DIM
QKV_CONV_WIDTH = 4
CHUNK = 64
POOL_WINDOWS = (2, 4, 8, 16)
POOL_DIM = 256
POOL_GROUP_DIM = 64
CONV_DIM = 256
CONV_WIDTH = 3
D_FF = 2816
AB_PAD = 128
REST_DIM = A_DIM + POOL_DIM + 3 * CONV_DIM + AB_PAD
Z_OFF, POOL_OFF, CB_OFF, CC_OFF, CH_OFF, AB_OFF = 0, 512, 768, 1024, 1280, 1536
D_IN_PAD = QKV_DIM + REST_DIM

QKV_HIST = 8
POOL_HIST = 16
VMEM_LIMIT_BYTES = 56 * 1024 * 1024
FFN_TILE = 512
FF_CHUNK = 256

BF16 = jnp.bfloat16
F32 = jnp.float32


def _sigmoid(x):
    return 1.0 / (1.0 + jnp.exp(-x))


def _silu(x):
    return x * _sigmoid(x)


def _softplus(x):
    return jnp.maximum(x, 0.0) + jnp.log1p(jnp.exp(-jnp.abs(x)))


def _rms_norm(x, g):
    return x * lax.rsqrt(jnp.mean(x * x, axis=-1, keepdims=True) + EPS) * g


def _bmm(a, b):
    return lax.dot_general(a.astype(BF16), b.astype(BF16), (((2,), (1,)), ((0,), (0,))),
                           preferred_element_type=F32)


def _bmm_nt(a, b):
    return lax.dot_general(a.astype(BF16), b.astype(BF16), (((2,), (2,)), ((0,), (0,))),
                           preferred_element_type=F32)


def _delta_chunk(qr, kr, v, gcol, grow, beta, state):
    c = CHUNK
    rq = lax.rsqrt(jnp.sum(qr * qr, axis=-1, keepdims=True) + EPS) * (HEAD_DIM ** -0.5)
    rk = lax.rsqrt(jnp.sum(kr * kr, axis=-1, keepdims=True) + EPS)
    q = qr * rq
    k = kr * rk
    eg = jnp.exp(gcol)
    g_last = gcol[:, c - 1:c, :]
    kb = k * beta
    vb = v * beta
    row = lax.broadcasted_iota(jnp.int32, (1, c, c), 1)
    col = lax.broadcasted_iota(jnp.int32, (1, c, c), 2)
    decay = jnp.exp(jnp.where(row >= col, gcol - grow, -1e30))
    kk = _bmm_nt(kb, k)
    qk = _bmm_nt(q, k)
    lower = jnp.where(row > col, kk * decay, 0.0)
    attn = qk * decay
    x = -lower
    t = jnp.where(row == col, 1.0, 0.0) + x
    n_sq = CHUNK.bit_length() - 2
    for _ in range(n_sq):
        x = _bmm(x, x)
        t = t + _bmm(t, x)
    u = _bmm(t, vb)
    w = _bmm(t, kb * eg)
    v_new = u - _bmm(w, state)
    o = _bmm(q * eg, state) + _bmm(attn, v_new)
    kd = k * jnp.exp(g_last - gcol)
    new_state = state * jnp.exp(g_last) + _bmm(jnp.swapaxes(kd, 1, 2), v_new)
    return o, new_state


def _mixer_kernel(x_ref, n1_ref, win_ref, cw_ref, alog_ref, dtb_ref, on_ref, pw_ref, ps_ref,
                  sw_ref, wout_ref, o_ref,
                  qkv_ext, pool_ext, m_ext, mixed_buf, state_ref):
    s = pl.program_id(0)
    nb = x_ref.shape[0]
    c = CHUNK

    @pl.when(s == 0)
    def _():
        qkv_ext[:, 0:QKV_HIST, :] = jnp.zeros((nb, QKV_HIST, QKV_DIM), F32)
        pool_ext[:, 0:POOL_HIST, :] = jnp.zeros((nb, POOL_HIST, POOL_DIM), F32)
        m_ext[:, 0:QKV_HIST, :] = jnp.zeros((nb, QKV_HIST, CONV_DIM), F32)
        state_ref[...] = jnp.zeros(state_ref.shape, F32)

    x = x_ref[...].reshape(nb * c, D_MODEL)
    h = _rms_norm(x, n1_ref[...]).astype(BF16)
    qkv_ext[:, QKV_HIST:QKV_HIST + c, :] = jnp.dot(
        h, win_ref[:, 0:QKV_DIM], preferred_element_type=F32).reshape(nb, c, QKV_DIM)
    rest = jnp.dot(h, win_ref[:, QKV_DIM:D_IN_PAD], preferred_element_type=F32).reshape(nb, c, REST_DIM)

    ext = qkv_ext[...]
    acc = ext * cw_ref[QKV_CONV_WIDTH - 1:QKV_CONV_WIDTH, :]
    for j in range(QKV_CONV_WIDTH - 1):
        acc = acc + pltpu.roll(ext, QKV_CONV_WIDTH - 1 - j, 1) * cw_ref[j:j + 1, :]
    act = _silu(acc[:, QKV_HIST:QKV_HIST + c, :])
    qkv_ext[:, 0:QKV_HIST, :] = ext[:, c:c + QKV_HIST, :]

    ab = rest[:, :, AB_OFF:AB_OFF + AB_PAD]
    g = -jnp.exp(alog_ref[...]) * _softplus(ab + dtb_ref[...])
    beta_all = _sigmoid(ab)
    row_c = lax.broadcasted_iota(jnp.int32, (1, c, AB_PAD), 1)
    gc = g
    sh = 1
    while sh < c:
        gc = gc + jnp.where(row_c >= sh, pltpu.roll(gc, sh, 1), 0.0)
        sh *= 2
    gct = jnp.swapaxes(gc, 1, 2)

    def heads(f):
        return jnp.concatenate([f(hd) for hd in range(HEADS)], axis=0)

    o, new_state = _delta_chunk(
        heads(lambda hd: act[:, :, hd * HEAD_DIM:(hd + 1) * HEAD_DIM]),
        heads(lambda hd: act[:, :, A_DIM + hd * HEAD_DIM:A_DIM + (hd + 1) * HEAD_DIM]),
        heads(lambda hd: act[:, :, 2 * A_DIM + hd * HEAD_DIM:2 * A_DIM + (hd + 1) * HEAD_DIM]),
        heads(lambda hd: gc[:, :, hd:hd + 1]),
        heads(lambda hd: gct[:, hd:hd + 1, :]),
        heads(lambda hd: beta_all[:, :, HEADS + hd:HEADS + hd + 1]),
        state_ref[...])
    state_ref[...] = new_state
    z = heads(lambda hd: rest[:, :, Z_OFF + hd * HEAD_DIM:Z_OFF + (hd + 1) * HEAD_DIM])
    o = o * lax.rsqrt(jnp.mean(o * o, axis=-1, keepdims=True) + EPS) * on_ref[...] * _silu(z)
    for hd in range(HEADS):
        mixed_buf[:, :, hd * HEAD_DIM:(hd + 1) * HEAD_DIM] = o[hd * nb:(hd + 1) * nb]

    hp = rest[:, :, POOL_OFF:POOL_OFF + POOL_DIM]
    pool_ext[:, POOL_HIST:POOL_HIST + c, :] = hp
    pext = pool_ext[...]
    s2 = pext + pltpu.roll(pext, 1, 1)
    s4 = s2 + pltpu.roll(s2, 2, 1)
    s8 = s4 + pltpu.roll(s4, 4, 1)
    s16 = s8 + pltpu.roll(s8, 8, 1)
    sl = slice(POOL_HIST, POOL_HIST + c)
    lane_p = lax.broadcasted_iota(jnp.int32, (1, c, POOL_DIM), 2)
    sums = jnp.where(lane_p < 64, s2[:, sl], jnp.where(lane_p < 128, s4[:, sl],
                     jnp.where(lane_p < 192, s8[:, sl], s16[:, sl])))
    pos = (s * c + 1 + lax.broadcasted_iota(jnp.int32, (1, c, POOL_DIM), 1)).astype(F32)
    win = jnp.where(lane_p < 64, 2.0, jnp.where(lane_p < 128, 4.0, jnp.where(lane_p < 192, 8.0, 16.0)))
    pooled = sums / jnp.minimum(pos, win) - hp
    y = jnp.dot(pooled.reshape(nb * c, POOL_DIM).astype(BF16), pw_ref[...],
                preferred_element_type=F32) * ps_ref[...]
    mixed_buf[:, :, A_DIM:A_DIM + POOL_DIM] = y.reshape(nb, c, POOL_DIM)
    pool_ext[:, 0:POOL_HIST, :] = pext[:, c:c + POOL_HIST, :]

    m_ext[:, QKV_HIST:QKV_HIST + c, :] = (rest[:, :, CC_OFF:CC_OFF + CONV_DIM]
                                          * rest[:, :, CH_OFF:CH_OFF + CONV_DIM])
    mext = m_ext[...]
    yc = (mext * sw_ref[2:3, :] + pltpu.roll(mext, 1, 1) * sw_ref[1:2, :]
          + pltpu.roll(mext, 2, 1) * sw_ref[0:1, :])
    mixed_buf[:, :, A_DIM + POOL_DIM:D_MODEL] = (rest[:, :, CB_OFF:CB_OFF + CONV_DIM]
                                                 * yc[:, QKV_HIST:QKV_HIST + c, :])
    m_ext[:, 0:QKV_HIST, :] = mext[:, c:c + QKV_HIST, :]

    mixed = mixed_buf[...].reshape(nb * c, D_MODEL).astype(BF16)
    out = x + jnp.dot(mixed, wout_ref[...], preferred_element_type=F32)
    o_ref[...] = out.reshape(nb, c, D_MODEL)


def _ffn_kernel(x_ref, p_ref, n2_ref, wg_ref, wu_ref, wd_ref, pg_ref, pp_ref, fg_ref, o_ref, ff_buf,
                *, final_norm):
    x = x_ref[...]
    h = _rms_norm(x, n2_ref[...]).astype(BF16)
    for ci in range(D_FF // FF_CHUNK):
        sl = slice(ci * FF_CHUNK, (ci + 1) * FF_CHUNK)
        gate = jnp.dot(h, wg_ref[:, sl], preferred_element_type=F32)
        up = jnp.dot(h, wu_ref[:, sl], preferred_element_type=F32)
        ff_buf[:, sl] = (_silu(gate) * up).astype(BF16)
    x = x + jnp.dot(ff_buf[...], wd_ref[...], preferred_element_type=F32)
    gate = _sigmoid(jnp.dot(x.astype(BF16), pg_ref[...], preferred_element_type=F32))
    emb = jnp.dot(p_ref[...].astype(BF16), pp_ref[...], preferred_element_type=F32)
    x = x + gate * emb
    if final_norm:
        x = _rms_norm(x, fg_ref[...])
    o_ref[...] = x


def _const_spec(shape):
    nd = len(shape)
    return pl.BlockSpec(shape, lambda *_: (0,) * nd, pipeline_mode=pl.Buffered(1))


def _mixer_call(x, n1, win, cw, alog, dtb, on, pw, ps, sw, wout):
    nb, seq, _ = x.shape
    c = CHUNK
    tile = pl.BlockSpec((nb, c, D_MODEL), lambda s: (0, s, 0))
    consts = (n1, win, cw, alog, dtb, on, pw, ps, sw, wout)
    return pl.pallas_call(
        _mixer_kernel,
        grid=(seq // c,),
        in_specs=[tile] + [_const_spec(a.shape) for a in consts],
        out_specs=tile,
        out_shape=jax.ShapeDtypeStruct(x.shape, F32),
        scratch_shapes=[
            pltpu.VMEM((nb, QKV_HIST + c, QKV_DIM), F32),
            pltpu.VMEM((nb, POOL_HIST + c, POOL_DIM), F32),
            pltpu.VMEM((nb, QKV_HIST + c, CONV_DIM), F32),
            pltpu.VMEM((nb, c, D_MODEL), F32),
            pltpu.VMEM((HEADS * nb, HEAD_DIM, HEAD_DIM), F32),
        ],
        compiler_params=pltpu.CompilerParams(
            dimension_semantics=("arbitrary",), vmem_limit_bytes=VMEM_LIMIT_BYTES),
        name="mixer",
    )(x, *consts)


def _ffn_call(x2d, p2d, n2, wg, wu, wd, pg, pp, fg, final_norm):
    tokens = x2d.shape[0]
    consts = (n2, wg, wu, wd, pg, pp, fg)
    return pl.pallas_call(
        functools.partial(_ffn_kernel, final_norm=final_norm),
        grid=(tokens // FFN_TILE,),
        in_specs=[pl.BlockSpec((FFN_TILE, D_MODEL), lambda i: (i, 0)),
                  pl.BlockSpec((FFN_TILE, PLE_DIM), lambda i: (i, 0))]
                 + [_const_spec(a.shape) for a in consts],
        out_specs=pl.BlockSpec((FFN_TILE, D_MODEL), lambda i: (i, 0)),
        out_shape=jax.ShapeDtypeStruct(x2d.shape, F32),
        scratch_shapes=[pltpu.VMEM((FFN_TILE, D_FF), BF16)],
        compiler_params=pltpu.CompilerParams(
            dimension_semantics=("arbitrary",), vmem_limit_bytes=VMEM_LIMIT_BYTES),
        name="ffn",
    )(x2d, p2d, *consts)


def _lane_row(vals, offset):
    return jnp.zeros((1, AB_PAD), F32).at[0, offset:offset + vals.shape[0]].set(vals.astype(F32))


def kernel(x, p, norm1_g, w_in, conv_qkv, a_log, dt_bias, onorm_g, pool_w, pool_scale, sconv_w, w_out,
           norm2_g, w_gate, w_up, w_down, ple_proj, ple_gate, final_g):
    depth = w_in.shape[0]
    nb, seq, _ = x.shape
    ab_lo = 4 * A_DIM
    rest_lo = ab_lo + 2 * HEADS
    for i in range(depth):
        wi = w_in[i]
        win = jnp.concatenate(
            [wi[:, :ab_lo], wi[:, rest_lo:], wi[:, ab_lo:rest_lo],
             jnp.zeros((D_MODEL, AB_PAD - 2 * HEADS), wi.dtype)], axis=1).astype(BF16)
        pw = jax.scipy.linalg.block_diag(*[pool_w[i, g] for g in range(len(POOL_WINDOWS))]).astype(BF16)
        x = _mixer_call(
            x, norm1_g[i][None, :], win, conv_qkv[i], _lane_row(a_log[i], 0), _lane_row(dt_bias[i], 0),
            onorm_g[i][None, :], pw, pool_scale[i][None, :], sconv_w[i], w_out[i].astype(BF16))
        x = _ffn_call(
            x.reshape(nb * seq, D_MODEL), p[i].reshape(nb * seq, PLE_DIM), norm2_g[i][None, :],
            w_gate[i].astype(BF16), w_up[i].astype(BF16), w_down[i].astype(BF16),
            ple_gate[i].astype(BF16), ple_proj[i].astype(BF16), final_g[None, :],
            final_norm=(i == depth - 1)).reshape(nb, seq, D_MODEL)
    return x
```

```python
import functools

import jax
import jax.numpy as jnp
from jax import lax
from jax.experimental import pallas as pl
from jax.experimental.pallas import tpu as pltpu

D_MODEL = 1024
PLE_DIM = 256
EPS = 1e-6
HEAD_DIM = 128
HEADS = 4
A_DIM = HEADS * HEAD_DIM
QKV_DIM = 3 * A_DIM
QKV_CONV_WIDTH = 4
CHUNK = 64
POOL_WINDOWS = (2, 4, 8, 16)
POOL_DIM = 256
POOL_GROUP_DIM = 64
CONV_DIM = 256
CONV_WIDTH = 3
D_FF = 2816
AB_PAD = 128
REST_DIM = A_DIM + POOL_DIM + 3 * CONV_DIM + AB_PAD
Z_OFF, POOL_OFF, CB_OFF, CC_OFF, CH_OFF, AB_OFF = 0, 512, 768, 1024, 1280, 1536
D_IN_PAD = QKV_DIM + REST_DIM
AB_LO = 4 * A_DIM
D_IN = AB_LO + 2 * HEADS + POOL_DIM + 3 * CONV_DIM
PREP_STEPS = 8

QKV_HIST = 8
POOL_HIST = 16
VMEM_LIMIT_BYTES = 56 * 1024 * 1024
FFN_TILE = 512
FF_CHUNK = 256
A_TILE = 256

BF16 = jnp.bfloat16
F32 = jnp.float32


def _sigmoid(x):
    return 0.5 * jnp.tanh(0.5 * x) + 0.5


def _silu(x):
    hx = 0.5 * x
    return hx + hx * jnp.tanh(hx)


def _softplus(x):
    return jnp.maximum(x, 0.0) + jnp.log1p(jnp.exp(-jnp.abs(x)))


def _rms_norm(x, g):
    return x * lax.rsqrt(jnp.mean(x * x, axis=-1, keepdims=True) + EPS) * g


def _bmm(a, b):
    return lax.dot_general(a.astype(BF16), b.astype(BF16), (((2,), (1,)), ((0,), (0,))),
                           preferred_element_type=F32)


def _bmm_nt(a, b):
    return lax.dot_general(a.astype(BF16), b.astype(BF16), (((2,), (2,)), ((0,), (0,))),
                           preferred_element_type=F32)


def _delta_chunk(qr, kr, v, gcol, grow, beta, state, filler):
    c = CHUNK
    rq = lax.rsqrt(jnp.sum(qr * qr, axis=-1, keepdims=True) + EPS) * (HEAD_DIM ** -0.5)
    rk = lax.rsqrt(jnp.sum(kr * kr, axis=-1, keepdims=True) + EPS)
    q = qr * rq
    k = kr * rk
    eg = jnp.exp(gcol)
    g_last = gcol[:, c - 1:c, :]
    kb = k * beta
    vb = v * beta
    row = lax.broadcasted_iota(jnp.int32, (1, c, c), 1)
    col = lax.broadcasted_iota(jnp.int32, (1, c, c), 2)
    decay = jnp.exp(jnp.where(row >= col, gcol - grow, -1e30))
    kq = _bmm_nt(jnp.concatenate([kb, q], axis=1), k)
    lower = jnp.where(row > col, kq[:, 0:c] * decay, 0.0)
    attn = kq[:, c:2 * c] * decay
    filler(1)
    x = -lower
    t = jnp.where(row == col, 1.0, 0.0) + x
    x = _bmm(x, x)
    n_sq = CHUNK.bit_length() - 2
    for _ in range(n_sq - 1):
        tx = _bmm(jnp.concatenate([x, t], axis=1), x)
        x = tx[:, 0:c]
        t = t + tx[:, c:2 * c]
        filler(1)
    t = t + _bmm(t, x)
    uw = _bmm(t, jnp.concatenate([vb, kb * eg], axis=2))
    ws = _bmm(jnp.concatenate([uw[:, :, HEAD_DIM:2 * HEAD_DIM], q * eg], axis=1), state)
    v_new = uw[:, :, 0:HEAD_DIM] - ws[:, 0:c]
    o = ws[:, c:2 * c] + _bmm(attn, v_new)
    kd = k * jnp.exp(g_last - gcol)
    new_state = state * jnp.exp(g_last) + _bmm(jnp.swapaxes(kd, 1, 2), v_new)
    return o, new_state


def _mixer_kernel(xn_ref, xc_ref, alog_ref, dtb_ref, n1_ref, win_ref, cw_ref, on_ref, pw_ref, ps_ref,
                  sw_ref, wout_ref, o_ref,
                  qkv_ext, rest_buf, pool_hist, m_hist, mixed_buf, state_ref, *, layer):
    s = pl.program_id(0)
    nb = xn_ref.shape[0]
    c = CHUNK

    @pl.when(s == 0)
    def _():
        qkv_ext[...] = jnp.zeros(qkv_ext.shape, F32)
        rest_buf[...] = jnp.zeros(rest_buf.shape, F32)
        pool_hist[...] = jnp.zeros(pool_hist.shape, F32)
        m_hist[...] = jnp.zeros(m_hist.shape, F32)
        state_ref[...] = jnp.zeros(state_ref.shape, F32)

    h = _rms_norm(xn_ref[...].reshape(nb * c, D_MODEL), n1_ref[...]).astype(BF16)
    a_tiles = iter(range(0, D_IN_PAD, A_TILE))

    def filler(n):
        for _ in range(n):
            lo = next(a_tiles, None)
            if lo is None:
                return
            hi = min(lo + A_TILE, D_IN_PAD)
            tile = jnp.dot(h, win_ref[:, lo:hi], preferred_element_type=F32).reshape(nb, c, hi - lo)
            if hi <= QKV_DIM:
                qkv_ext[:, QKV_HIST:QKV_HIST + c, lo:hi] = tile
            else:
                rest_buf[:, :, lo - QKV_DIM:hi - QKV_DIM] = tile

    rest = rest_buf[...]

    acc = qkv_ext[:, QKV_HIST:QKV_HIST + c, :] * cw_ref[QKV_CONV_WIDTH - 1:QKV_CONV_WIDTH, :]
    for j in range(QKV_CONV_WIDTH - 1):
        lo = QKV_HIST - (QKV_CONV_WIDTH - 1 - j)
        acc = acc + qkv_ext[:, lo:lo + c, :] * cw_ref[j:j + 1, :]
    qkv_ext[:, 0:QKV_HIST, :] = qkv_ext[:, c:c + QKV_HIST, :]
    filler(3)
    act = _silu(acc)
    filler(3)

    hp = rest[:, :, POOL_OFF:POOL_OFF + POOL_DIM]
    pext = jnp.concatenate([pool_hist[...], hp], axis=1)
    s2 = pext + pltpu.roll(pext, 1, 1)
    s4 = s2 + pltpu.roll(s2, 2, 1)
    s8 = s4 + pltpu.roll(s4, 4, 1)
    s16 = s8 + pltpu.roll(s8, 8, 1)
    sl = slice(POOL_HIST, POOL_HIST + c)
    lane_p = lax.broadcasted_iota(jnp.int32, (1, c, POOL_DIM), 2)
    sums = jnp.where(lane_p < 64, s2[:, sl], jnp.where(lane_p < 128, s4[:, sl],
                     jnp.where(lane_p < 192, s8[:, sl], s16[:, sl])))
    pos = ((s - 1) * c + 1 + lax.broadcasted_iota(jnp.int32, (1, c, POOL_DIM), 1)).astype(F32)
    win = jnp.where(lane_p < 64, 2.0, jnp.where(lane_p < 128, 4.0, jnp.where(lane_p < 192, 8.0, 16.0)))
    pooled = sums / jnp.maximum(jnp.minimum(pos, win), 1.0) - hp
    y = jnp.dot(pooled.reshape(nb * c, POOL_DIM).astype(BF16), pw_ref[...],
                preferred_element_type=F32) * ps_ref[...]
    mixed_buf[:, :, A_DIM:A_DIM + POOL_DIM] = y.reshape(nb, c, POOL_DIM)
    pool_hist[...] = pext[:, c:c + POOL_HIST, :]

    mext = jnp.concatenate([m_hist[...], rest[:, :, CC_OFF:CC_OFF + CONV_DIM]
                            * rest[:, :, CH_OFF:CH_OFF + CONV_DIM]], axis=1)
    yc = (mext * sw_ref[2:3, :] + pltpu.roll(mext, 1, 1) * sw_ref[1:2, :]
          + pltpu.roll(mext, 2, 1) * sw_ref[0:1, :])
    mixed_buf[:, :, A_DIM + POOL_DIM:D_MODEL] = (rest[:, :, CB_OFF:CB_OFF + CONV_DIM]
                                                 * yc[:, QKV_HIST:QKV_HIST + c, :])
    m_hist[...] = mext[:, c:c + QKV_HIST, :]
    filler(2)

    ab = rest[:, :, AB_OFF:AB_OFF + AB_PAD]
    lane_ab = lax.broadcasted_iota(jnp.int32, (1, AB_PAD), 1)
    alog = jnp.zeros((1, AB_PAD), F32)
    dtb = jnp.zeros((1, AB_PAD), F32)
    for hd in range(HEADS):
        alog = jnp.where(lane_ab == hd, alog_ref[layer, hd], alog)
        dtb = jnp.where(lane_ab == hd, dtb_ref[layer, hd], dtb)
    g = -jnp.exp(alog) * _softplus(ab + dtb)
    beta_all = _sigmoid(ab)
    row_c = lax.broadcasted_iota(jnp.int32, (1, c, AB_PAD), 1)
    gc = g
    sh = 1
    while sh < c:
        gc = gc + jnp.where(row_c >= sh, pltpu.roll(gc, sh, 1), 0.0)
        sh *= 2
    gct = jnp.swapaxes(gc, 1, 2)

    def heads(f):
        return jnp.concatenate([f(hd) for hd in range(HEADS)], axis=0)

    out_gate = on_ref[...] * _silu(
        heads(lambda hd: rest[:, :, Z_OFF + hd * HEAD_DIM:Z_OFF + (hd + 1) * HEAD_DIM]))
    filler(2)
    o, new_state = _delta_chunk(
        heads(lambda hd: act[:, :, hd * HEAD_DIM:(hd + 1) * HEAD_DIM]),
        heads(lambda hd: act[:, :, A_DIM + hd * HEAD_DIM:A_DIM + (hd + 1) * HEAD_DIM]),
        heads(lambda hd: act[:, :, 2 * A_DIM + hd * HEAD_DIM:2 * A_DIM + (hd + 1) * HEAD_DIM]),
        heads(lambda hd: gc[:, :, hd:hd + 1]),
        heads(lambda hd: gct[:, hd:hd + 1, :]),
        heads(lambda hd: beta_all[:, :, HEADS + hd:HEADS + hd + 1]),
        state_ref[...], filler)
    state_ref[...] = new_state
    o = o * lax.rsqrt(jnp.mean(o * o, axis=-1, keepdims=True) + EPS) * out_gate
    for hd in range(HEADS):
        mixed_buf[:, :, hd * HEAD_DIM:(hd + 1) * HEAD_DIM] = o[hd * nb:(hd + 1) * nb]
    filler(D_IN_PAD // A_TILE + 1)

    mixed = mixed_buf[...].reshape(nb * c, D_MODEL).astype(BF16)
    out = xc_ref[...].reshape(nb * c, D_MODEL) + jnp.dot(mixed, wout_ref[...], preferred_element_type=F32)
    o_ref[...] = out.reshape(nb, c, D_MODEL)


def _ffn_kernel(x_ref, p_ref, n2_ref, wg_ref, wu_ref, wd_ref, pg_ref, pp_ref, fg_ref, o_ref, ff_buf,
                *, final_norm):
    x = x_ref[...]
    h = _rms_norm(x, n2_ref[...]).astype(BF16)
    for ci in range(D_FF // FF_CHUNK):
        sl = slice(ci * FF_CHUNK, (ci + 1) * FF_CHUNK)
        gate = jnp.dot(h, wg_ref[:, sl], preferred_element_type=F32)
        up = jnp.dot(h, wu_ref[:, sl], preferred_element_type=F32)
        ff_buf[:, sl] = (_silu(gate) * up).astype(BF16)
    x = x + jnp.dot(ff_buf[...], wd_ref[...], preferred_element_type=F32)
    gate = _sigmoid(jnp.dot(x.astype(BF16), pg_ref[...], preferred_element_type=F32))
    emb = jnp.dot(p_ref[...].astype(BF16), pp_ref[...], preferred_element_type=F32)
    x = x + gate * emb
    if final_norm:
        x = _rms_norm(x, fg_ref[...])
    o_ref[...] = x


def _prep_kernel(win_ref, wout_ref, wg_ref, wu_ref, wd_ref, pg_ref, pp_ref, pw_ref,
                 win_o, wout_o, wg_o, wu_o, wd_o, pg_o, pp_o, pw_o):
    w = win_ref[...]
    rows = w.shape[0]
    win_o[:, 0:AB_LO] = w[:, 0:AB_LO].astype(BF16)
    win_o[:, AB_LO:D_IN_PAD - AB_PAD] = w[:, AB_LO + 2 * HEADS:D_IN].astype(BF16)
    win_o[:, D_IN_PAD - AB_PAD:D_IN_PAD] = jnp.concatenate(
        [w[:, AB_LO:AB_LO + 2 * HEADS], jnp.zeros((rows, AB_PAD - 2 * HEADS), F32)], axis=1).astype(BF16)
    wout_o[...] = wout_ref[...].astype(BF16)
    wg_o[...] = wg_ref[...].astype(BF16)
    wu_o[...] = wu_ref[...].astype(BF16)
    wd_o[...] = wd_ref[...].astype(BF16)
    pg_o[...] = pg_ref[...].astype(BF16)
    pp_o[...] = pp_ref[...].astype(BF16)
    r = lax.broadcasted_iota(jnp.int32, (POOL_GROUP_DIM, POOL_DIM), 0)
    l = lax.broadcasted_iota(jnp.int32, (POOL_GROUP_DIM, POOL_DIM), 1)
    spread = jnp.where(l % POOL_GROUP_DIM == r, 1.0, 0.0).astype(BF16)
    tiled = jnp.dot(pw_ref[...].astype(BF16), spread, preferred_element_type=F32)
    rr = lax.broadcasted_iota(jnp.int32, (POOL_DIM, POOL_DIM), 0)
    ll = lax.broadcasted_iota(jnp.int32, (POOL_DIM, POOL_DIM), 1)
    pw_o[...] = jnp.where(rr // POOL_GROUP_DIM == ll // POOL_GROUP_DIM, tiled, 0.0).astype(BF16)


def _prep_call(w_in, w_out, w_gate, w_up, w_down, ple_gate, ple_proj, pool_w):
    depth = w_in.shape[0]
    pool_w = pool_w.reshape(depth, POOL_DIM, POOL_GROUP_DIM)
    ins = (w_in, w_out, w_gate, w_up, w_down, ple_gate, ple_proj)
    out_cols = (D_IN_PAD,) + tuple(a.shape[2] for a in ins[1:])

    def row_spec(a, cols):
        return pl.BlockSpec((None, a.shape[1] // PREP_STEPS, cols), lambda l, r: (l, r, 0))

    return pl.pallas_call(
        _prep_kernel,
        grid=(depth, PREP_STEPS),
        in_specs=[row_spec(a, a.shape[2]) for a in ins]
                 + [pl.BlockSpec((None, POOL_DIM, POOL_GROUP_DIM), lambda l, r: (l, 0, 0))],
        out_specs=[row_spec(a, n) for a, n in zip(ins, out_cols)]
                  + [pl.BlockSpec((None, POOL_DIM, POOL_DIM), lambda l, r: (l, 0, 0))],
        out_shape=[jax.ShapeDtypeStruct((depth, a.shape[1], n), BF16) for a, n in zip(ins, out_cols)]
                  + [jax.ShapeDtypeStruct((depth, POOL_DIM, POOL_DIM), BF16)],
        compiler_params=pltpu.CompilerParams(
            dimension_semantics=("arbitrary", "arbitrary"), vmem_limit_bytes=VMEM_LIMIT_BYTES),
        name="prep",
    )(*ins, pool_w)


def _layer_spec(a, layer):
    nd = a.ndim - 1
    return pl.BlockSpec((None,) + a.shape[1:], lambda *_: (layer,) + (0,) * nd,
                        pipeline_mode=pl.Buffered(1))


_SMEM_SPEC = pl.BlockSpec(memory_space=pltpu.SMEM)


def _mixer_call(x, layer, n1, win, cw, alog, dtb, on, pw, ps, sw, wout):
    nb, seq, _ = x.shape
    c = CHUNK
    n_chunks = seq // c
    nxt = pl.BlockSpec((nb, c, D_MODEL), lambda s: (0, jnp.minimum(s, n_chunks - 1), 0))
    cur = pl.BlockSpec((nb, c, D_MODEL), lambda s: (0, jnp.maximum(s - 1, 0), 0))
    vmem_consts = (n1, win, cw, on, pw, ps, sw, wout)
    return pl.pallas_call(
        functools.partial(_mixer_kernel, layer=layer),
        grid=(n_chunks + 1,),
        in_specs=[nxt, cur, _SMEM_SPEC, _SMEM_SPEC] + [_layer_spec(a, layer) for a in vmem_consts],
        out_specs=cur,
        out_shape=jax.ShapeDtypeStruct(x.shape, F32),
        scratch_shapes=[
            pltpu.VMEM((nb, QKV_HIST + c, QKV_DIM), F32),
            pltpu.VMEM((nb, c, REST_DIM), F32),
            pltpu.VMEM((nb, POOL_HIST, POOL_DIM), F32),
            pltpu.VMEM((nb, QKV_HIST, CONV_DIM), F32),
            pltpu.VMEM((nb, c, D_MODEL), F32),
            pltpu.VMEM((HEADS * nb, HEAD_DIM, HEAD_DIM), F32),
        ],
        compiler_params=pltpu.CompilerParams(
            dimension_semantics=("arbitrary",), vmem_limit_bytes=VMEM_LIMIT_BYTES),
        name="mixer",
    )(x, x, alog, dtb, *vmem_consts)


def _ffn_call(x2d, p3d, layer, n2, wg, wu, wd, pg, pp, fg, final_norm):
    tokens = x2d.shape[0]
    consts = (n2, wg, wu, wd, pg, pp)
    return pl.pallas_call(
        functools.partial(_ffn_kernel, final_norm=final_norm),
        grid=(tokens // FFN_TILE,),
        in_specs=[pl.BlockSpec((FFN_TILE, D_MODEL), lambda i: (i, 0)),
                  pl.BlockSpec((None, FFN_TILE, PLE_DIM), lambda i: (layer, i, 0))]
                 + [_layer_spec(a, layer) for a in consts]
                 + [pl.BlockSpec(fg.shape, lambda i: (0, 0), pipeline_mode=pl.Buffered(1))],
        out_specs=pl.BlockSpec((FFN_TILE, D_MODEL), lambda i: (i, 0)),
        out_shape=jax.ShapeDtypeStruct(x2d.shape, F32),
        scratch_shapes=[pltpu.VMEM((FFN_TILE, D_FF), BF16)],
        compiler_params=pltpu.CompilerParams(
            dimension_semantics=("arbitrary",), vmem_limit_bytes=VMEM_LIMIT_BYTES),
        name="ffn",
    )(x2d, p3d, *consts, fg)


def kernel(x, p, norm1_g, w_in, conv_qkv, a_log, dt_bias, onorm_g, pool_w, pool_scale, sconv_w, w_out,
           norm2_g, w_gate, w_up, w_down, ple_proj, ple_gate, final_g):
    depth = w_in.shape[0]
    nb, seq, _ = x.shape
    win, wout, wg, wu, wd, pg, pp, pw = _prep_call(
        w_in, w_out, w_gate, w_up, w_down, ple_gate, ple_proj, pool_w)
    n1, on, ps, n2 = (a.reshape(depth, 1, a.shape[-1]) for a in (norm1_g, onorm_g, pool_scale, norm2_g))
    p3d = p.reshape(depth, nb * seq, PLE_DIM)
    fg = final_g.reshape(1, D_MODEL)
    for i in range(depth):
        x = _mixer_call(x, i, n1, win, conv_qkv, a_log, dt_bias, on, pw, ps, sconv_w, wout)
        x = _ffn_call(x.reshape(nb * seq, D_MODEL), p3d, i, n2, wg, wu, wd, pg, pp, fg,
                      final_norm=(i == depth - 1)).reshape(nb, seq, D_MODEL)
    return x
```

```python
import functools

import jax
import jax.numpy as jnp
from jax import lax
from jax.experimental import pallas as pl
from jax.experimental.pallas import tpu as pltpu

D_MODEL = 1024
PLE_DIM = 256
EPS = 1e-6
HEAD_DIM = 128
HEADS = 4
A_DIM = HEADS * HEAD_DIM
QKV_DIM = 3 * A_DIM
QKV_CONV_WIDTH = 4
CHUNK = 64
POOL_WINDOWS = (2, 4, 8, 16)
POOL_DIM = 256
POOL_GROUP_DIM = 64
CONV_DIM = 256
CONV_WIDTH = 3
D_FF = 2816
AB_PAD = 128
REST_DIM = A_DIM + POOL_DIM + 3 * CONV_DIM + AB_PAD
Z_OFF, POOL_OFF, CB_OFF, CC_OFF, CH_OFF, AB_OFF = 0, 512, 768, 1024, 1280, 1536
D_IN_PAD = QKV_DIM + REST_DIM
AB_LO = 4 * A_DIM
D_IN = AB_LO + 2 * HEADS + POOL_DIM + 3 * CONV_DIM
PREP_STEPS = 8

QKV_HIST = 8
POOL_HIST = 16
VMEM_LIMIT_BYTES = 56 * 1024 * 1024
FFN_TILE = 512
FF_CHUNK = 256
A_TILE = 256

BF16 = jnp.bfloat16
F32 = jnp.float32


def _sigmoid(x):
    return 0.5 * jnp.tanh(0.5 * x) + 0.5


def _silu(x):
    hx = 0.5 * x
    return hx + hx * jnp.tanh(hx)


def _softplus(x):
    return jnp.maximum(x, 0.0) + jnp.log1p(jnp.exp(-jnp.abs(x)))


def _rms_norm(x, g):
    return x * lax.rsqrt(jnp.mean(x * x, axis=-1, keepdims=True) + EPS) * g


def _bmm(a, b):
    return lax.dot_general(a.astype(BF16), b.astype(BF16), (((2,), (1,)), ((0,), (0,))),
                           preferred_element_type=F32)


def _bmm_nt(a, b):
    return lax.dot_general(a.astype(BF16), b.astype(BF16), (((2,), (2,)), ((0,), (0,))),
                           preferred_element_type=F32)


def _delta_chunk(qr, kr, v, gcol, grow, beta, state, filler):
    c = CHUNK
    rq = lax.rsqrt(jnp.sum(qr * qr, axis=-1, keepdims=True) + EPS) * (HEAD_DIM ** -0.5)
    rk = lax.rsqrt(jnp.sum(kr * kr, axis=-1, keepdims=True) + EPS)
    q = qr * rq
    k = kr * rk
    eg = jnp.exp(gcol)
    g_last = gcol[:, c - 1:c, :]
    kb = k * beta
    vb = v * beta
    row = lax.broadcasted_iota(jnp.int32, (1, c, c), 1)
    col = lax.broadcasted_iota(jnp.int32, (1, c, c), 2)
    decay = jnp.exp(jnp.where(row >= col, gcol - grow, -1e30))
    kq = _bmm_nt(jnp.concatenate([kb, q], axis=1), k)
    lower = jnp.where(row > col, kq[:, 0:c] * decay, 0.0)
    attn = kq[:, c:2 * c] * decay
    filler(1)
    x = -lower
    t = jnp.where(row == col, 1.0, 0.0) + x
    x = _bmm(x, x)
    n_sq = CHUNK.bit_length() - 2
    for _ in range(n_sq - 1):
        tx = _bmm(jnp.concatenate([x, t], axis=1), x)
        x = tx[:, 0:c]
        t = t + tx[:, c:2 * c]
        filler(1)
    t = t + _bmm(t, x)
    uw = _bmm(t, jnp.concatenate([vb, kb * eg], axis=2))
    ws = _bmm(jnp.concatenate([uw[:, :, HEAD_DIM:2 * HEAD_DIM], q * eg], axis=1), state)
    v_new = uw[:, :, 0:HEAD_DIM] - ws[:, 0:c]
    o = ws[:, c:2 * c] + _bmm(attn, v_new)
    kd = k * jnp.exp(g_last - gcol)
    new_state = state * jnp.exp(g_last) + _bmm(jnp.swapaxes(kd, 1, 2), v_new)
    return o, new_state


def _mixer_kernel(xn_ref, xc_ref, alog_ref, dtb_ref, n1_ref, win_ref, cw_ref, on_ref, pw_ref, ps_ref,
                  sw_ref, wout_ref, o_ref,
                  h_buf, qkv_ext, rest_buf, pool_hist, m_hist, mixed_buf, state_ref, *, layer):
    s = pl.program_id(0)
    nb = xn_ref.shape[0]
    c = CHUNK

    @pl.when(s == 0)
    def _():
        qkv_ext[...] = jnp.zeros(qkv_ext.shape, F32)
        rest_buf[...] = jnp.zeros(rest_buf.shape, F32)
        pool_hist[...] = jnp.zeros(pool_hist.shape, F32)
        m_hist[...] = jnp.zeros(m_hist.shape, F32)
        state_ref[...] = jnp.zeros(state_ref.shape, F32)
        h_buf[...] = _rms_norm(xc_ref[...].reshape(nb * c, D_MODEL), n1_ref[...]).astype(BF16)

    rest_tiles = [QKV_DIM + off for off in (POOL_OFF, CB_OFF, CC_OFF, CH_OFF, AB_OFF, Z_OFF, Z_OFF + A_TILE)]
    a_tiles = iter(list(range(0, QKV_DIM, A_TILE)) + rest_tiles)

    def filler(n):
        for _ in range(n):
            lo = next(a_tiles, None)
            if lo is None:
                return
            hi = min(lo + A_TILE, D_IN_PAD)
            tile = lax.dot_general(h_buf[...], win_ref[lo:hi, :], (((1,), (1,)), ((), ())),
                                   preferred_element_type=F32).reshape(nb, c, hi - lo)
            if hi <= QKV_DIM:
                qkv_ext[:, QKV_HIST:QKV_HIST + c, lo:hi] = tile
            else:
                rest_buf[:, :, lo - QKV_DIM:hi - QKV_DIM] = tile


    act_tiles = []
    for lo_c in range(0, QKV_DIM, A_TILE):
        cols = slice(lo_c, lo_c + A_TILE)
        acc = qkv_ext[:, QKV_HIST:QKV_HIST + c, cols] * cw_ref[QKV_CONV_WIDTH - 1:QKV_CONV_WIDTH, cols]
        for j in range(QKV_CONV_WIDTH - 1):
            lo = QKV_HIST - (QKV_CONV_WIDTH - 1 - j)
            acc = acc + qkv_ext[:, lo:lo + c, cols] * cw_ref[j:j + 1, cols]
        qkv_ext[:, 0:QKV_HIST, cols] = qkv_ext[:, c:c + QKV_HIST, cols]
        filler(1)
        act_tiles.append(_silu(acc))
    act = jnp.concatenate(act_tiles, axis=2)

    hp = rest_buf[:, :, POOL_OFF:POOL_OFF + POOL_DIM]
    pext = jnp.concatenate([pool_hist[...], hp], axis=1)
    s2 = pext + pltpu.roll(pext, 1, 1)
    s4 = s2 + pltpu.roll(s2, 2, 1)
    s8 = s4 + pltpu.roll(s4, 4, 1)
    s16 = s8 + pltpu.roll(s8, 8, 1)
    sl = slice(POOL_HIST, POOL_HIST + c)
    lane_p = lax.broadcasted_iota(jnp.int32, (1, c, POOL_DIM), 2)
    sums = jnp.where(lane_p < 64, s2[:, sl], jnp.where(lane_p < 128, s4[:, sl],
                     jnp.where(lane_p < 192, s8[:, sl], s16[:, sl])))
    pos = ((s - 1) * c + 1 + lax.broadcasted_iota(jnp.int32, (1, c, POOL_DIM), 1)).astype(F32)
    win = jnp.where(lane_p < 64, 2.0, jnp.where(lane_p < 128, 4.0, jnp.where(lane_p < 192, 8.0, 16.0)))
    pooled = sums / jnp.maximum(jnp.minimum(pos, win), 1.0) - hp
    y = jnp.dot(pooled.reshape(nb * c, POOL_DIM).astype(BF16), pw_ref[...],
                preferred_element_type=F32) * ps_ref[...]
    mixed_buf[:, :, A_DIM:A_DIM + POOL_DIM] = y.reshape(nb, c, POOL_DIM)
    pool_hist[...] = pext[:, c:c + POOL_HIST, :]
    filler(1)

    mext = jnp.concatenate([m_hist[...], rest_buf[:, :, CC_OFF:CC_OFF + CONV_DIM]
                            * rest_buf[:, :, CH_OFF:CH_OFF + CONV_DIM]], axis=1)
    yc = (mext * sw_ref[2:3, :] + pltpu.roll(mext, 1, 1) * sw_ref[1:2, :]
          + pltpu.roll(mext, 2, 1) * sw_ref[0:1, :])
    mixed_buf[:, :, A_DIM + POOL_DIM:D_MODEL] = (rest_buf[:, :, CB_OFF:CB_OFF + CONV_DIM]
                                                 * yc[:, QKV_HIST:QKV_HIST + c, :])
    m_hist[...] = mext[:, c:c + QKV_HIST, :]
    filler(3)

    ab = rest_buf[:, :, AB_OFF:AB_OFF + AB_PAD]
    filler(1)
    lane_ab = lax.broadcasted_iota(jnp.int32, (1, AB_PAD), 1)
    alog = jnp.zeros((1, AB_PAD), F32)
    dtb = jnp.zeros((1, AB_PAD), F32)
    for hd in range(HEADS):
        alog = jnp.where(lane_ab == hd, alog_ref[layer, hd], alog)
        dtb = jnp.where(lane_ab == hd, dtb_ref[layer, hd], dtb)
    g = -jnp.exp(alog) * _softplus(ab + dtb)
    beta_all = _sigmoid(ab)
    row_c = lax.broadcasted_iota(jnp.int32, (1, c, AB_PAD), 1)
    gc = g
    sh = 1
    while sh < c:
        gc = gc + jnp.where(row_c >= sh, pltpu.roll(gc, sh, 1), 0.0)
        sh *= 2
    gct = jnp.swapaxes(gc, 1, 2)

    def heads(f):
        return jnp.concatenate([f(hd) for hd in range(HEADS)], axis=0)

    out_gate = on_ref[...] * _silu(
        heads(lambda hd: rest_buf[:, :, Z_OFF + hd * HEAD_DIM:Z_OFF + (hd + 1) * HEAD_DIM]))
    filler(2)
    o, new_state = _delta_chunk(
        heads(lambda hd: act[:, :, hd * HEAD_DIM:(hd + 1) * HEAD_DIM]),
        heads(lambda hd: act[:, :, A_DIM + hd * HEAD_DIM:A_DIM + (hd + 1) * HEAD_DIM]),
        heads(lambda hd: act[:, :, 2 * A_DIM + hd * HEAD_DIM:2 * A_DIM + (hd + 1) * HEAD_DIM]),
        heads(lambda hd: gc[:, :, hd:hd + 1]),
        heads(lambda hd: gct[:, hd:hd + 1, :]),
        heads(lambda hd: beta_all[:, :, HEADS + hd:HEADS + hd + 1]),
        state_ref[...], filler)
    state_ref[...] = new_state
    o = o * lax.rsqrt(jnp.mean(o * o, axis=-1, keepdims=True) + EPS) * out_gate
    for hd in range(HEADS):
        mixed_buf[:, :, hd * HEAD_DIM:(hd + 1) * HEAD_DIM] = o[hd * nb:(hd + 1) * nb]
    filler(D_IN_PAD // A_TILE + 1)

    h_buf[...] = _rms_norm(xn_ref[...].reshape(nb * c, D_MODEL), n1_ref[...]).astype(BF16)

    mixed = mixed_buf[...].reshape(nb * c, D_MODEL).astype(BF16)
    out = xc_ref[...].reshape(nb * c, D_MODEL) + jnp.dot(mixed, wout_ref[...], preferred_element_type=F32)
    o_ref[...] = out.reshape(nb, c, D_MODEL)


def _ffn_kernel(x_ref, p_ref, n2_ref, wg_ref, wu_ref, wd_ref, pg_ref, pp_ref, fg_ref, o_ref, ff_buf,
                *, final_norm):
    x = x_ref[...]
    h = _rms_norm(x, n2_ref[...]).astype(BF16)
    for ci in range(D_FF // FF_CHUNK):
        sl = slice(ci * FF_CHUNK, (ci + 1) * FF_CHUNK)
        gate = jnp.dot(h, wg_ref[:, sl], preferred_element_type=F32)
        up = jnp.dot(h, wu_ref[:, sl], preferred_element_type=F32)
        ff_buf[:, sl] = (_silu(gate) * up).astype(BF16)
    x = x + jnp.dot(ff_buf[...], wd_ref[...], preferred_element_type=F32)
    gate = _sigmoid(jnp.dot(x.astype(BF16), pg_ref[...], preferred_element_type=F32))
    emb = jnp.dot(p_ref[...].astype(BF16), pp_ref[...], preferred_element_type=F32)
    x = x + gate * emb
    if final_norm:
        x = _rms_norm(x, fg_ref[...])
    o_ref[...] = x


def _prep_win_kernel(win_ref, win_o):
    win_o[0:AB_LO, :] = win_ref[0:AB_LO, :].astype(BF16)
    win_o[AB_LO:D_IN_PAD - AB_PAD, :] = win_ref[AB_LO + 2 * HEADS:D_IN, :].astype(BF16)
    win_o[D_IN_PAD - AB_PAD:D_IN_PAD, :] = jnp.concatenate(
        [win_ref[AB_LO:AB_LO + 2 * HEADS, :], jnp.zeros((AB_PAD - 2 * HEADS, D_MODEL), F32)],
        axis=0).astype(BF16)


def _prep_kernel(wout_ref, wg_ref, wu_ref, wd_ref, pg_ref, pp_ref, pw_ref,
                 wout_o, wg_o, wu_o, wd_o, pg_o, pp_o, pw_o):
    wout_o[...] = wout_ref[...].astype(BF16)
    wg_o[...] = wg_ref[...].astype(BF16)
    wu_o[...] = wu_ref[...].astype(BF16)
    wd_o[...] = wd_ref[...].astype(BF16)
    pg_o[...] = pg_ref[...].astype(BF16)
    pp_o[...] = pp_ref[...].astype(BF16)
    r = lax.broadcasted_iota(jnp.int32, (POOL_GROUP_DIM, POOL_DIM), 0)
    l = lax.broadcasted_iota(jnp.int32, (POOL_GROUP_DIM, POOL_DIM), 1)
    spread = jnp.where(l % POOL_GROUP_DIM == r, 1.0, 0.0).astype(BF16)
    tiled = jnp.dot(pw_ref[...].astype(BF16), spread, preferred_element_type=F32)
    rr = lax.broadcasted_iota(jnp.int32, (POOL_DIM, POOL_DIM), 0)
    ll = lax.broadcasted_iota(jnp.int32, (POOL_DIM, POOL_DIM), 1)
    pw_o[...] = jnp.where(rr // POOL_GROUP_DIM == ll // POOL_GROUP_DIM, tiled, 0.0).astype(BF16)


def _prep_call(w_in, w_out, w_gate, w_up, w_down, ple_gate, ple_proj, pool_w):
    depth = w_in.shape[0]
    params = pltpu.CompilerParams(dimension_semantics=("arbitrary", "arbitrary"),
                                  vmem_limit_bytes=VMEM_LIMIT_BYTES)
    w_in_t = jnp.swapaxes(w_in, 1, 2)
    win = pl.pallas_call(
        _prep_win_kernel,
        grid=(depth, 1),
        in_specs=[pl.BlockSpec((None, D_IN, D_MODEL), lambda l, r: (l, 0, 0))],
        out_specs=pl.BlockSpec((None, D_IN_PAD, D_MODEL), lambda l, r: (l, 0, 0)),
        out_shape=jax.ShapeDtypeStruct((depth, D_IN_PAD, D_MODEL), BF16),
        compiler_params=params,
        name="prep_win",
    )(w_in_t)

    pool_w = pool_w.reshape(depth, POOL_DIM, POOL_GROUP_DIM)
    ins = (w_out, w_gate, w_up, w_down, ple_gate, ple_proj)

    def row_spec(a):
        return pl.BlockSpec((None, a.shape[1] // PREP_STEPS, a.shape[2]), lambda l, r: (l, r, 0))

    outs = pl.pallas_call(
        _prep_kernel,
        grid=(depth, PREP_STEPS),
        in_specs=[row_spec(a) for a in ins]
                 + [pl.BlockSpec((None, POOL_DIM, POOL_GROUP_DIM), lambda l, r: (l, 0, 0))],
        out_specs=[row_spec(a) for a in ins]
                  + [pl.BlockSpec((None, POOL_DIM, POOL_DIM), lambda l, r: (l, 0, 0))],
        out_shape=[jax.ShapeDtypeStruct(a.shape, BF16) for a in ins]
                  + [jax.ShapeDtypeStruct((depth, POOL_DIM, POOL_DIM), BF16)],
        compiler_params=params,
        name="prep",
    )(*ins, pool_w)
    return (win, *outs)


def _layer_spec(a, layer):
    nd = a.ndim - 1
    return pl.BlockSpec((None,) + a.shape[1:], lambda *_: (layer,) + (0,) * nd,
                        pipeline_mode=pl.Buffered(1))


_SMEM_SPEC = pl.BlockSpec(memory_space=pltpu.SMEM)


def _mixer_call(x, layer, n1, win, cw, alog, dtb, on, pw, ps, sw, wout):
    nb, seq, _ = x.shape
    c = CHUNK
    n_chunks = seq // c
    nxt = pl.BlockSpec((nb, c, D_MODEL), lambda s: (0, jnp.minimum(s + 1, n_chunks - 1), 0))
    cur = pl.BlockSpec((nb, c, D_MODEL), lambda s: (0, jnp.maximum(s - 1, 0), 0))
    vmem_consts = (n1, win, cw, on, pw, ps, sw, wout)
    return pl.pallas_call(
        functools.partial(_mixer_kernel, layer=layer),
        grid=(n_chunks + 1,),
        in_specs=[nxt, cur, _SMEM_SPEC, _SMEM_SPEC] + [_layer_spec(a, layer) for a in vmem_consts],
        out_specs=cur,
        out_shape=jax.ShapeDtypeStruct(x.shape, F32),
        scratch_shapes=[
            pltpu.VMEM((nb * c, D_MODEL), BF16),
            pltpu.VMEM((nb, QKV_HIST + c, QKV_DIM), F32),
            pltpu.VMEM((nb, c, REST_DIM), F32),
            pltpu.VMEM((nb, POOL_HIST, POOL_DIM), F32),
            pltpu.VMEM((nb, QKV_HIST, CONV_DIM), F32),
            pltpu.VMEM((nb, c, D_MODEL), F32),
            pltpu.VMEM((HEADS * nb, HEAD_DIM, HEAD_DIM), F32),
        ],
        compiler_params=pltpu.CompilerParams(
            dimension_semantics=("arbitrary",), vmem_limit_bytes=VMEM_LIMIT_BYTES),
        name="mixer",
    )(x, x, alog, dtb, *vmem_consts)


def _ffn_call(x2d, p3d, layer, n2, wg, wu, wd, pg, pp, fg, final_norm):
    tokens = x2d.shape[0]
    consts = (n2, wg, wu, wd, pg, pp)
    return pl.pallas_call(
        functools.partial(_ffn_kernel, final_norm=final_norm),
        grid=(tokens // FFN_TILE,),
        in_specs=[pl.BlockSpec((FFN_TILE, D_MODEL), lambda i: (i, 0)),
                  pl.BlockSpec((None, FFN_TILE, PLE_DIM), lambda i: (layer, i, 0))]
                 + [_layer_spec(a, layer) for a in consts]
                 + [pl.BlockSpec(fg.shape, lambda i: (0, 0), pipeline_mode=pl.Buffered(1))],
        out_specs=pl.BlockSpec((FFN_TILE, D_MODEL), lambda i: (i, 0)),
        out_shape=jax.ShapeDtypeStruct(x2d.shape, F32),
        scratch_shapes=[pltpu.VMEM((FFN_TILE, D_FF), BF16)],
        compiler_params=pltpu.CompilerParams(
            dimension_semantics=("arbitrary",), vmem_limit_bytes=VMEM_LIMIT_BYTES),
        name="ffn",
    )(x2d, p3d, *consts, fg)


def kernel(x, p, norm1_g, w_in, conv_qkv, a_log, dt_bias, onorm_g, pool_w, pool_scale, sconv_w, w_out,
           norm2_g, w_gate, w_up, w_down, ple_proj, ple_gate, final_g):
    depth = w_in.shape[0]
    nb, seq, _ = x.shape
    win, wout, wg, wu, wd, pg, pp, pw = _prep_call(
        w_in, w_out, w_gate, w_up, w_down, ple_gate, ple_proj, pool_w)
    n1, on, ps, n2 = (a.reshape(depth, 1, a.shape[-1]) for a in (norm1_g, onorm_g, pool_scale, norm2_g))
    p3d = p.reshape(depth, nb * seq, PLE_DIM)
    fg = final_g.reshape(1, D_MODEL)
    for i in range(depth):
        x = _mixer_call(x, i, n1, win, conv_qkv, a_log, dt_bias, on, pw, ps, sconv_w, wout)
        x = _ffn_call(x.reshape(nb * seq, D_MODEL), p3d, i, n2, wg, wu, wd, pg, pp, fg,
                      final_norm=(i == depth - 1)).reshape(nb, seq, D_MODEL)
    return x
```

```python
import functools

import jax
import jax.numpy as jnp
from jax import lax
from jax.experimental import pallas as pl
from jax.experimental.pallas import tpu as pltpu

D_MODEL = 1024
PLE_DIM = 256
EPS = 1e-6
HEAD_DIM = 128
HEADS = 4
A_DIM = HEADS * HEAD_DIM
QKV_DIM = 3 * A_DIM
QKV_CONV_WIDTH = 4
CHUNK = 64
POOL_WINDOWS = (2, 4, 8, 16)
POOL_DIM = 256
POOL_GROUP_DIM = 64
CONV_DIM = 256
CONV_WIDTH = 3
D_FF = 2816
AB_PAD = 128
REST_DIM = A_DIM + POOL_DIM + 3 * CONV_DIM + AB_PAD
Z_OFF, POOL_OFF, CB_OFF, CC_OFF, CH_OFF, AB_OFF = 0, 512, 768, 1024, 1280, 1536
D_IN_PAD = QKV_DIM + REST_DIM
AB_LO = 4 * A_DIM
D_IN = AB_LO + 2 * HEADS + POOL_DIM + 3 * CONV_DIM
PREP_STEPS = 8

QKV_HIST = 8
POOL_HIST = 16
VMEM_LIMIT_BYTES = 56 * 1024 * 1024
FFN_TILE = 512
FF_CHUNK = 256
A_TILE = 256

BF16 = jnp.bfloat16
F32 = jnp.float32


def _sigmoid(x):
    return 0.5 * jnp.tanh(0.5 * x) + 0.5


def _silu(x):
    hx = 0.5 * x
    return hx + hx * jnp.tanh(hx)


def _softplus(x):
    return jnp.maximum(x, 0.0) + jnp.log1p(jnp.exp(-jnp.abs(x)))


def _rms_norm(x, g):
    return x * lax.rsqrt(jnp.mean(x * x, axis=-1, keepdims=True) + EPS) * g


def _bmm(a, b):
    return lax.dot_general(a.astype(BF16), b.astype(BF16), (((2,), (1,)), ((0,), (0,))),
                           preferred_element_type=F32)


def _bmm_nt(a, b):
    return lax.dot_general(a.astype(BF16), b.astype(BF16), (((2,), (2,)), ((0,), (0,))),
                           preferred_element_type=F32)


def _delta_chunk(qr, kr, v, gc, beta, state, filler):
    c = CHUNK
    nb = gc.shape[0]
    wide = HEADS * c

    def heads(f):
        return jnp.concatenate([f(hd) for hd in range(HEADS)], axis=0)

    def on_lanes(a):
        return jnp.concatenate([a[hd * nb:(hd + 1) * nb] for hd in range(HEADS)], axis=2)

    def block_diag(a):
        zero = jnp.zeros((nb,) + a.shape[1:], a.dtype)
        return jnp.concatenate(
            [jnp.concatenate([a[hd * nb:(hd + 1) * nb] if other == hd else zero
                              for other in range(HEADS)], axis=2) for hd in range(HEADS)], axis=1)

    gcol = heads(lambda hd: gc[:, :, hd:hd + 1])
    rq = lax.rsqrt(jnp.sum(qr * qr, axis=-1, keepdims=True) + EPS) * (HEAD_DIM ** -0.5)
    rk = lax.rsqrt(jnp.sum(kr * kr, axis=-1, keepdims=True) + EPS)
    q = qr * rq
    k = kr * rk
    eg = jnp.exp(gcol)
    g_last = gcol[:, c - 1:c, :]
    kb = k * beta
    vb = v * beta

    row = lax.broadcasted_iota(jnp.int32, (1, c, wide), 1)
    lane = lax.broadcasted_iota(jnp.int32, (1, c, wide), 2)
    blk = lane // c
    col = lane - blk * c
    gcol_w = gc[:, :, HEADS - 1:HEADS]
    for hd in range(HEADS - 2, -1, -1):
        gcol_w = jnp.where(blk == hd, gc[:, :, hd:hd + 1], gcol_w)
    grow_w = jnp.sum(jnp.where(row == col, gcol_w, 0.0), axis=1, keepdims=True)
    decay = jnp.exp(jnp.where(row >= col, gcol_w - grow_w, -1e30))

    kq = _bmm_nt(jnp.concatenate([on_lanes(kb), on_lanes(q)], axis=1).astype(BF16),
                 block_diag(k.astype(BF16)))
    lower = jnp.where(row > col, kq[:, 0:c] * decay, 0.0)
    attn = kq[:, c:2 * c] * decay
    filler(1)

    row_b = lax.broadcasted_iota(jnp.int32, (1, wide, wide), 1) // c
    lane_b = lax.broadcasted_iota(jnp.int32, (1, wide, wide), 2) // c

    def diag_rhs(a):
        a = a.astype(BF16)
        return jnp.where(row_b == lane_b, jnp.concatenate([a] * HEADS, axis=1), jnp.zeros((), BF16))

    x = -lower
    t = jnp.where(row == col, 1.0, 0.0) + x
    x = _bmm(x, diag_rhs(x))
    n_sq = CHUNK.bit_length() - 2
    for _ in range(n_sq - 1):
        tx = _bmm(jnp.concatenate([x, t], axis=1), diag_rhs(x))
        x = tx[:, 0:c]
        t = t + tx[:, c:2 * c]
        filler(1)
    t = t + _bmm(t, diag_rhs(x))

    t_h = heads(lambda hd: t[:, :, hd * c:(hd + 1) * c])
    uw = _bmm(t_h, jnp.concatenate([vb, kb * eg], axis=2))
    ws = _bmm(jnp.concatenate([uw[:, :, HEAD_DIM:2 * HEAD_DIM], q * eg], axis=1), state)
    v_new = uw[:, :, 0:HEAD_DIM] - ws[:, 0:c]
    av = _bmm(attn, block_diag(v_new.astype(BF16)))
    o = ws[:, c:2 * c] + heads(lambda hd: av[:, :, hd * HEAD_DIM:(hd + 1) * HEAD_DIM])
    kd = k * jnp.exp(g_last - gcol)
    new_state = state * jnp.exp(g_last) + _bmm(jnp.swapaxes(kd, 1, 2), v_new)
    return o, new_state


def _mixer_kernel(xn_ref, xc_ref, alog_ref, dtb_ref, n1_ref, win_ref, cw_ref, on_ref, pw_ref, ps_ref,
                  sw_ref, wout_ref, o_ref,
                  h_buf, qkv_ext, rest_buf, pool_hist, m_hist, mixed_buf, state_ref, *, layer):
    s = pl.program_id(0)
    nb = xn_ref.shape[0]
    c = CHUNK

    @pl.when(s == 0)
    def _():
        qkv_ext[...] = jnp.zeros(qkv_ext.shape, F32)
        rest_buf[...] = jnp.zeros(rest_buf.shape, F32)
        pool_hist[...] = jnp.zeros(pool_hist.shape, F32)
        m_hist[...] = jnp.zeros(m_hist.shape, F32)
        state_ref[...] = jnp.zeros(state_ref.shape, F32)
        h_buf[...] = _rms_norm(xc_ref[...].reshape(nb * c, D_MODEL), n1_ref[...]).astype(BF16)

    rest_tiles = [QKV_DIM + off for off in (POOL_OFF, CB_OFF, CC_OFF, CH_OFF, AB_OFF, Z_OFF, Z_OFF + A_TILE)]
    a_tiles = iter(list(range(0, QKV_DIM, A_TILE)) + rest_tiles)

    def filler(n):
        for _ in range(n):
            lo = next(a_tiles, None)
            if lo is None:
                return
            hi = min(lo + A_TILE, D_IN_PAD)
            tile = lax.dot_general(h_buf[...], win_ref[lo:hi, :], (((1,), (1,)), ((), ())),
                                   preferred_element_type=F32).reshape(nb, c, hi - lo)
            if hi <= QKV_DIM:
                qkv_ext[:, QKV_HIST:QKV_HIST + c, lo:hi] = tile
            else:
                rest_buf[:, :, lo - QKV_DIM:hi - QKV_DIM] = tile


    act_tiles = []
    for lo_c in range(0, QKV_DIM, A_TILE):
        cols = slice(lo_c, lo_c + A_TILE)
        acc = qkv_ext[:, QKV_HIST:QKV_HIST + c, cols] * cw_ref[QKV_CONV_WIDTH - 1:QKV_CONV_WIDTH, cols]
        for j in range(QKV_CONV_WIDTH - 1):
            lo = QKV_HIST - (QKV_CONV_WIDTH - 1 - j)
            acc = acc + qkv_ext[:, lo:lo + c, cols] * cw_ref[j:j + 1, cols]
        qkv_ext[:, 0:QKV_HIST, cols] = qkv_ext[:, c:c + QKV_HIST, cols]
        filler(1)
        act_tiles.append(_silu(acc))
    act = jnp.concatenate(act_tiles, axis=2)

    hp = rest_buf[:, :, POOL_OFF:POOL_OFF + POOL_DIM]
    pext = jnp.concatenate([pool_hist[...], hp], axis=1)
    s2 = pext + pltpu.roll(pext, 1, 1)
    s4 = s2 + pltpu.roll(s2, 2, 1)
    s8 = s4 + pltpu.roll(s4, 4, 1)
    s16 = s8 + pltpu.roll(s8, 8, 1)
    sl = slice(POOL_HIST, POOL_HIST + c)
    lane_p = lax.broadcasted_iota(jnp.int32, (1, c, POOL_DIM), 2)
    sums = jnp.where(lane_p < 64, s2[:, sl], jnp.where(lane_p < 128, s4[:, sl],
                     jnp.where(lane_p < 192, s8[:, sl], s16[:, sl])))
    pos = ((s - 1) * c + 1 + lax.broadcasted_iota(jnp.int32, (1, c, POOL_DIM), 1)).astype(F32)
    win = jnp.where(lane_p < 64, 2.0, jnp.where(lane_p < 128, 4.0, jnp.where(lane_p < 192, 8.0, 16.0)))
    pooled = sums / jnp.maximum(jnp.minimum(pos, win), 1.0) - hp
    y = jnp.dot(pooled.reshape(nb * c, POOL_DIM).astype(BF16), pw_ref[...],
                preferred_element_type=F32) * ps_ref[...]
    mixed_buf[:, :, A_DIM:A_DIM + POOL_DIM] = y.reshape(nb, c, POOL_DIM)
    pool_hist[...] = pext[:, c:c + POOL_HIST, :]
    filler(1)

    mext = jnp.concatenate([m_hist[...], rest_buf[:, :, CC_OFF:CC_OFF + CONV_DIM]
                            * rest_buf[:, :, CH_OFF:CH_OFF + CONV_DIM]], axis=1)
    yc = (mext * sw_ref[2:3, :] + pltpu.roll(mext, 1, 1) * sw_ref[1:2, :]
          + pltpu.roll(mext, 2, 1) * sw_ref[0:1, :])
    mixed_buf[:, :, A_DIM + POOL_DIM:D_MODEL] = (rest_buf[:, :, CB_OFF:CB_OFF + CONV_DIM]
                                                 * yc[:, QKV_HIST:QKV_HIST + c, :])
    m_hist[...] = mext[:, c:c + QKV_HIST, :]
    filler(3)

    ab = rest_buf[:, :, AB_OFF:AB_OFF + AB_PAD]
    lane_ab = lax.broadcasted_iota(jnp.int32, (1, AB_PAD), 1)
    alog = jnp.zeros((1, AB_PAD), F32)
    dtb = jnp.zeros((1, AB_PAD), F32)
    for hd in range(HEADS):
        alog = jnp.where(lane_ab == hd, alog_ref[layer, hd], alog)
        dtb = jnp.where(lane_ab == hd, dtb_ref[layer, hd], dtb)
    g = -jnp.exp(alog) * _softplus(ab + dtb)
    beta_all = _sigmoid(ab)
    row_c = lax.broadcasted_iota(jnp.int32, (1, c, AB_PAD), 1)
    gc = g
    sh = 1
    while sh < c:
        gc = gc + jnp.where(row_c >= sh, pltpu.roll(gc, sh, 1), 0.0)
        sh *= 2

    def heads(f):
        return jnp.concatenate([f(hd) for hd in range(HEADS)], axis=0)

    out_gate = on_ref[...] * _silu(
        heads(lambda hd: rest_buf[:, :, Z_OFF + hd * HEAD_DIM:Z_OFF + (hd + 1) * HEAD_DIM]))
    o, new_state = _delta_chunk(
        heads(lambda hd: act[:, :, hd * HEAD_DIM:(hd + 1) * HEAD_DIM]),
        heads(lambda hd: act[:, :, A_DIM + hd * HEAD_DIM:A_DIM + (hd + 1) * HEAD_DIM]),
        heads(lambda hd: act[:, :, 2 * A_DIM + hd * HEAD_DIM:2 * A_DIM + (hd + 1) * HEAD_DIM]),
        gc,
        heads(lambda hd: beta_all[:, :, HEADS + hd:HEADS + hd + 1]),
        state_ref[...], filler)
    state_ref[...] = new_state
    o = o * lax.rsqrt(jnp.mean(o * o, axis=-1, keepdims=True) + EPS) * out_gate
    for hd in range(HEADS):
        mixed_buf[:, :, hd * HEAD_DIM:(hd + 1) * HEAD_DIM] = o[hd * nb:(hd + 1) * nb]
    filler(D_IN_PAD // A_TILE + 1)

    h_buf[...] = _rms_norm(xn_ref[...].reshape(nb * c, D_MODEL), n1_ref[...]).astype(BF16)

    mixed = mixed_buf[...].reshape(nb * c, D_MODEL).astype(BF16)
    out = xc_ref[...].reshape(nb * c, D_MODEL) + jnp.dot(mixed, wout_ref[...], preferred_element_type=F32)
    o_ref[...] = out.reshape(nb, c, D_MODEL)


def _ffn_kernel(x_ref, p_ref, n2_ref, wg_ref, wu_ref, wd_ref, pg_ref, pp_ref, fg_ref, o_ref, ff_buf,
                *, final_norm):
    x = x_ref[...]
    h = _rms_norm(x, n2_ref[...]).astype(BF16)
    for ci in range(D_FF // FF_CHUNK):
        sl = slice(ci * FF_CHUNK, (ci + 1) * FF_CHUNK)
        gate = jnp.dot(h, wg_ref[:, sl], preferred_element_type=F32)
        up = jnp.dot(h, wu_ref[:, sl], preferred_element_type=F32)
        ff_buf[:, sl] = (_silu(gate) * up).astype(BF16)
    x = x + jnp.dot(ff_buf[...], wd_ref[...], preferred_element_type=F32)
    gate = _sigmoid(jnp.dot(x.astype(BF16), pg_ref[...], preferred_element_type=F32))
    emb = jnp.dot(p_ref[...].astype(BF16), pp_ref[...], preferred_element_type=F32)
    x = x + gate * emb
    if final_norm:
        x = _rms_norm(x, fg_ref[...])
    o_ref[...] = x


def _prep_win_kernel(win_ref, win_o):
    win_o[0:AB_LO, :] = win_ref[0:AB_LO, :].astype(BF16)
    win_o[AB_LO:D_IN_PAD - AB_PAD, :] = win_ref[AB_LO + 2 * HEADS:D_IN, :].astype(BF16)
    win_o[D_IN_PAD - AB_PAD:D_IN_PAD, :] = jnp.concatenate(
        [win_ref[AB_LO:AB_LO + 2 * HEADS, :], jnp.zeros((AB_PAD - 2 * HEADS, D_MODEL), F32)],
        axis=0).astype(BF16)


def _prep_kernel(wout_ref, wg_ref, wu_ref, wd_ref, pg_ref, pp_ref, pw_ref,
                 wout_o, wg_o, wu_o, wd_o, pg_o, pp_o, pw_o):
    wout_o[...] = wout_ref[...].astype(BF16)
    wg_o[...] = wg_ref[...].astype(BF16)
    wu_o[...] = wu_ref[...].astype(BF16)
    wd_o[...] = wd_ref[...].astype(BF16)
    pg_o[...] = pg_ref[...].astype(BF16)
    pp_o[...] = pp_ref[...].astype(BF16)
    r = lax.broadcasted_iota(jnp.int32, (POOL_GROUP_DIM, POOL_DIM), 0)
    l = lax.broadcasted_iota(jnp.int32, (POOL_GROUP_DIM, POOL_DIM), 1)
    spread = jnp.where(l % POOL_GROUP_DIM == r, 1.0, 0.0).astype(BF16)
    tiled = jnp.dot(pw_ref[...].astype(BF16), spread, preferred_element_type=F32)
    rr = lax.broadcasted_iota(jnp.int32, (POOL_DIM, POOL_DIM), 0)
    ll = lax.broadcasted_iota(jnp.int32, (POOL_DIM, POOL_DIM), 1)
    pw_o[...] = jnp.where(rr // POOL_GROUP_DIM == ll // POOL_GROUP_DIM, tiled, 0.0).astype(BF16)


def _prep_call(w_in, w_out, w_gate, w_up, w_down, ple_gate, ple_proj, pool_w):
    depth = w_in.shape[0]
    params = pltpu.CompilerParams(dimension_semantics=("arbitrary", "arbitrary"),
                                  vmem_limit_bytes=VMEM_LIMIT_BYTES)
    w_in_t = jnp.swapaxes(w_in, 1, 2)
    win = pl.pallas_call(
        _prep_win_kernel,
        grid=(depth, 1),
        in_specs=[pl.BlockSpec((None, D_IN, D_MODEL), lambda l, r: (l, 0, 0))],
        out_specs=pl.BlockSpec((None, D_IN_PAD, D_MODEL), lambda l, r: (l, 0, 0)),
        out_shape=jax.ShapeDtypeStruct((depth, D_IN_PAD, D_MODEL), BF16),
        compiler_params=params,
        name="prep_win",
    )(w_in_t)

    pool_w = pool_w.reshape(depth, POOL_DIM, POOL_GROUP_DIM)
    ins = (w_out, w_gate, w_up, w_down, ple_gate, ple_proj)

    def row_spec(a):
        return pl.BlockSpec((None, a.shape[1] // PREP_STEPS, a.shape[2]), lambda l, r: (l, r, 0))

    outs = pl.pallas_call(
        _prep_kernel,
        grid=(depth, PREP_STEPS),
        in_specs=[row_spec(a) for a in ins]
                 + [pl.BlockSpec((None, POOL_DIM, POOL_GROUP_DIM), lambda l, r: (l, 0, 0))],
        out_specs=[row_spec(a) for a in ins]
                  + [pl.BlockSpec((None, POOL_DIM, POOL_DIM), lambda l, r: (l, 0, 0))],
        out_shape=[jax.ShapeDtypeStruct(a.shape, BF16) for a in ins]
                  + [jax.ShapeDtypeStruct((depth, POOL_DIM, POOL_DIM), BF16)],
        compiler_params=params,
        name="prep",
    )(*ins, pool_w)
    return (win, *outs)


def _layer_spec(a, layer):
    nd = a.ndim - 1
    return pl.BlockSpec((None,) + a.shape[1:], lambda *_: (layer,) + (0,) * nd,
                        pipeline_mode=pl.Buffered(1))


_SMEM_SPEC = pl.BlockSpec(memory_space=pltpu.SMEM)


def _mixer_call(x, layer, n1, win, cw, alog, dtb, on, pw, ps, sw, wout):
    nb, seq, _ = x.shape
    c = CHUNK
    n_chunks = seq // c
    nxt = pl.BlockSpec((nb, c, D_MODEL), lambda s: (0, jnp.minimum(s + 1, n_chunks - 1), 0))
    cur = pl.BlockSpec((nb, c, D_MODEL), lambda s: (0, jnp.maximum(s - 1, 0), 0))
    vmem_consts = (n1, win, cw, on, pw, ps, sw, wout)
    return pl.pallas_call(
        functools.partial(_mixer_kernel, layer=layer),
        grid=(n_chunks + 1,),
        in_specs=[nxt, cur, _SMEM_SPEC, _SMEM_SPEC] + [_layer_spec(a, layer) for a in vmem_consts],
        out_specs=cur,
        out_shape=jax.ShapeDtypeStruct(x.shape, F32),
        scratch_shapes=[
            pltpu.VMEM((nb * c, D_MODEL), BF16),
            pltpu.VMEM((nb, QKV_HIST + c, QKV_DIM), F32),
            pltpu.VMEM((nb, c, REST_DIM), F32),
            pltpu.VMEM((nb, POOL_HIST, POOL_DIM), F32),
            pltpu.VMEM((nb, QKV_HIST, CONV_DIM), F32),
            pltpu.VMEM((nb, c, D_MODEL), F32),
            pltpu.VMEM((HEADS * nb, HEAD_DIM, HEAD_DIM), F32),
        ],
        compiler_params=pltpu.CompilerParams(
            dimension_semantics=("arbitrary",), vmem_limit_bytes=VMEM_LIMIT_BYTES),
        name="mixer",
    )(x, x, alog, dtb, *vmem_consts)


def _ffn_call(x2d, p3d, layer, n2, wg, wu, wd, pg, pp, fg, final_norm):
    tokens = x2d.shape[0]
    consts = (n2, wg, wu, wd, pg, pp)
    return pl.pallas_call(
        functools.partial(_ffn_kernel, final_norm=final_norm),
        grid=(tokens // FFN_TILE,),
        in_specs=[pl.BlockSpec((FFN_TILE, D_MODEL), lambda i: (i, 0)),
                  pl.BlockSpec((None, FFN_TILE, PLE_DIM), lambda i: (layer, i, 0))]
                 + [_layer_spec(a, layer) for a in consts]
                 + [pl.BlockSpec(fg.shape, lambda i: (0, 0), pipeline_mode=pl.Buffered(1))],
        out_specs=pl.BlockSpec((FFN_TILE, D_MODEL), lambda i: (i, 0)),
        out_shape=jax.ShapeDtypeStruct(x2d.shape, F32),
        scratch_shapes=[pltpu.VMEM((FFN_TILE, D_FF), BF16)],
        compiler_params=pltpu.CompilerParams(
            dimension_semantics=("arbitrary",), vmem_limit_bytes=VMEM_LIMIT_BYTES),
        name="ffn",
    )(x2d, p3d, *consts, fg)


def kernel(x, p, norm1_g, w_in, conv_qkv, a_log, dt_bias, onorm_g, pool_w, pool_scale, sconv_w, w_out,
           norm2_g, w_gate, w_up, w_down, ple_proj, ple_gate, final_g):
    depth = w_in.shape[0]
    nb, seq, _ = x.shape
    win, wout, wg, wu, wd, pg, pp, pw = _prep_call(
        w_in, w_out, w_gate, w_up, w_down, ple_gate, ple_proj, pool_w)
    n1, on, ps, n2 = (a.reshape(depth, 1, a.shape[-1]) for a in (norm1_g, onorm_g, pool_scale, norm2_g))
    p3d = p.reshape(depth, nb * seq, PLE_DIM)
    fg = final_g.reshape(1, D_MODEL)
    for i in range(depth):
        x = _mixer_call(x, i, n1, win, conv_qkv, a_log, dt_bias, on, pw, ps, sconv_w, wout)
        x = _ffn_call(x.reshape(nb * seq, D_MODEL), p3d, i, n2, wg, wu, wd, pg, pp, fg,
                      final_norm=(i == depth - 1)).reshape(nb, seq, D_MODEL)
    return x
```

```python
import functools

import jax
import jax.numpy as jnp
from jax import lax
from jax.experimental import pallas as pl
from jax.experimental.pallas import tpu as pltpu

D_MODEL = 1024
PLE_DIM = 256
EPS = 1e-6
HEAD_DIM = 128
HEADS = 4
A_DIM = HEADS * HEAD_DIM
QKV_DIM = 3 * A_DIM
QKV_CONV_WIDTH = 4
CHUNK = 64
POOL_WINDOWS = (2, 4, 8, 16)
POOL_DIM = 256
POOL_GROUP_DIM = 64
CONV_DIM = 256
CONV_WIDTH = 3
D_FF = 2816
AB_PAD = 128
REST_DIM = A_DIM + POOL_DIM + 3 * CONV_DIM + AB_PAD
Z_OFF, POOL_OFF, CB_OFF, CC_OFF, CH_OFF, AB_OFF = 0, 512, 768, 1024, 1280, 1536
D_IN_PAD = QKV_DIM + REST_DIM
AB_LO = 4 * A_DIM
D_IN = AB_LO + 2 * HEADS + POOL_DIM + 3 * CONV_DIM

QKV_HIST = 8
POOL_HIST = 16
VMEM_LIMIT_BYTES = 56 * 1024 * 1024
FFN_TILE = 1024
FF_CHUNK = 256
A_TILE = 256

BF16 = jnp.bfloat16
F32 = jnp.float32


def _sigmoid(x):
    return 0.5 * jnp.tanh(0.5 * x) + 0.5


def _silu(x):
    hx = 0.5 * x
    return hx + hx * jnp.tanh(hx)


def _softplus(x):
    return jnp.maximum(x, 0.0) + jnp.log1p(jnp.exp(-jnp.abs(x)))


def _rms_norm(x, g):
    return x * lax.rsqrt(jnp.mean(x * x, axis=-1, keepdims=True) + EPS) * g


def _bmm(a, b):
    return lax.dot_general(a.astype(BF16), b.astype(BF16), (((2,), (1,)), ((0,), (0,))),
                           preferred_element_type=F32)


def _bmm_nt(a, b):
    return lax.dot_general(a.astype(BF16), b.astype(BF16), (((2,), (2,)), ((0,), (0,))),
                           preferred_element_type=F32)


def _delta_chunk(qr, kr, v, gc, beta, state, filler):
    c = CHUNK
    nb = gc.shape[0]
    wide = HEADS * c

    def heads(f):
        return jnp.concatenate([f(hd) for hd in range(HEADS)], axis=0)

    def on_lanes(a):
        return jnp.concatenate([a[hd * nb:(hd + 1) * nb] for hd in range(HEADS)], axis=2)

    def block_diag(a):
        zero = jnp.zeros((nb,) + a.shape[1:], a.dtype)
        return jnp.concatenate(
            [jnp.concatenate([a[hd * nb:(hd + 1) * nb] if other == hd else zero
                              for other in range(HEADS)], axis=2) for hd in range(HEADS)], axis=1)

    gcol = heads(lambda hd: gc[:, :, hd:hd + 1])
    rq = lax.rsqrt(jnp.sum(qr * qr, axis=-1, keepdims=True) + EPS) * (HEAD_DIM ** -0.5)
    rk = lax.rsqrt(jnp.sum(kr * kr, axis=-1, keepdims=True) + EPS)
    eg = jnp.exp(gcol)
    g_last = gcol[:, c - 1:c, :]
    kf = kr * rk
    kbf = kf * beta
    q = (qr * rq).astype(BF16)
    qg = (qr * (rq * eg)).astype(BF16)
    k = kf.astype(BF16)
    kb = kbf.astype(BF16)
    kbg = (kbf * eg).astype(BF16)
    kd = (kf * jnp.exp(g_last - gcol)).astype(BF16)
    vb = (v * beta).astype(BF16)

    row = lax.broadcasted_iota(jnp.int32, (1, c, wide), 1)
    lane = lax.broadcasted_iota(jnp.int32, (1, c, wide), 2)
    blk = lane // c
    col = lane - blk * c
    gcol_w = gc[:, :, HEADS - 1:HEADS]
    for hd in range(HEADS - 2, -1, -1):
        gcol_w = jnp.where(blk == hd, gc[:, :, hd:hd + 1], gcol_w)
    grow_w = jnp.sum(jnp.where(row == col, gcol_w, 0.0), axis=1, keepdims=True)
    decay = jnp.exp(jnp.where(row >= col, gcol_w - grow_w, -1e30))

    kq = _bmm_nt(jnp.concatenate([on_lanes(kb), on_lanes(q)], axis=1), block_diag(k))
    lower = jnp.where(row > col, kq[:, 0:c] * decay, 0.0)
    attn = (kq[:, c:2 * c] * decay).astype(BF16)
    filler(1)

    row_b = lax.broadcasted_iota(jnp.int32, (1, wide, wide), 1) // c
    lane_b = lax.broadcasted_iota(jnp.int32, (1, wide, wide), 2) // c

    def diag_rhs(a):
        return jnp.where(row_b == lane_b, jnp.concatenate([a] * HEADS, axis=1), jnp.zeros((), BF16))

    xf = -lower
    t = jnp.where(row == col, 1.0, 0.0) + xf
    x = xf.astype(BF16)
    x = _bmm(x, diag_rhs(x)).astype(BF16)
    n_sq = CHUNK.bit_length() - 2
    for _ in range(n_sq - 1):
        tx = _bmm(jnp.concatenate([x, t.astype(BF16)], axis=1), diag_rhs(x))
        x = tx[:, 0:c].astype(BF16)
        t = t + tx[:, c:2 * c]
        filler(1)
    t = (t + _bmm(t, diag_rhs(x))).astype(BF16)

    t_h = heads(lambda hd: t[:, :, hd * c:(hd + 1) * c])
    uw = _bmm(t_h, jnp.concatenate([vb, kbg], axis=2))
    ws = _bmm(jnp.concatenate([uw[:, :, HEAD_DIM:2 * HEAD_DIM].astype(BF16), qg], axis=1), state)
    v_new = (uw[:, :, 0:HEAD_DIM] - ws[:, 0:c]).astype(BF16)
    av = _bmm(attn, block_diag(v_new))
    o = ws[:, c:2 * c] + heads(lambda hd: av[:, :, hd * HEAD_DIM:(hd + 1) * HEAD_DIM])
    new_state = state * jnp.exp(g_last) + _bmm(jnp.swapaxes(kd, 1, 2), v_new)
    return o, new_state


def _mixer_kernel(xn_ref, xc_ref, alog_ref, dtb_ref, n1_ref, win_ref, cw_ref, on_ref, pw_ref, ps_ref,
                  sw_ref, wout_ref, wg_ref, wu_ref, wd_ref, pg_ref, pp_ref,
                  o_ref, wg_o, wu_o, wd_o, pg_o, pp_o,
                  h_buf, qkv_ext, rest_buf, pool_hist, m_hist, mixed_buf, state_ref, *, layer):
    s = pl.program_id(0)
    nb = xn_ref.shape[0]
    c = CHUNK

    for src_ref, dst_ref in ((wg_ref, wg_o), (wu_ref, wu_o), (wd_ref, wd_o), (pg_ref, pg_o), (pp_ref, pp_o)):
        dst_ref[...] = src_ref[...].astype(BF16)

    @pl.when(s == 0)
    def _():
        qkv_ext[...] = jnp.zeros(qkv_ext.shape, F32)
        rest_buf[...] = jnp.zeros(rest_buf.shape, F32)
        pool_hist[...] = jnp.zeros(pool_hist.shape, F32)
        m_hist[...] = jnp.zeros(m_hist.shape, F32)
        state_ref[...] = jnp.zeros(state_ref.shape, F32)
        h_buf[...] = _rms_norm(xc_ref[...].reshape(nb * c, D_MODEL), n1_ref[...]).astype(BF16)

    rest_tiles = [QKV_DIM + off for off in (POOL_OFF, CB_OFF, CC_OFF, CH_OFF, AB_OFF, Z_OFF, Z_OFF + A_TILE)]
    a_tiles = iter(list(range(0, QKV_DIM, A_TILE)) + rest_tiles)

    def filler(n):
        for _ in range(n):
            lo = next(a_tiles, None)
            if lo is None:
                return
            hi = min(lo + A_TILE, D_IN_PAD)
            tile = lax.dot_general(h_buf[...], win_ref[lo:hi, :], (((1,), (1,)), ((), ())),
                                   preferred_element_type=F32).reshape(nb, c, hi - lo)
            if hi <= QKV_DIM:
                qkv_ext[:, QKV_HIST:QKV_HIST + c, lo:hi] = tile
            else:
                rest_buf[:, :, lo - QKV_DIM:hi - QKV_DIM] = tile


    act_tiles = []
    for lo_c in range(0, QKV_DIM, A_TILE):
        cols = slice(lo_c, lo_c + A_TILE)
        acc = qkv_ext[:, QKV_HIST:QKV_HIST + c, cols] * cw_ref[QKV_CONV_WIDTH - 1:QKV_CONV_WIDTH, cols]
        for j in range(QKV_CONV_WIDTH - 1):
            lo = QKV_HIST - (QKV_CONV_WIDTH - 1 - j)
            acc = acc + qkv_ext[:, lo:lo + c, cols] * cw_ref[j:j + 1, cols]
        qkv_ext[:, 0:QKV_HIST, cols] = qkv_ext[:, c:c + QKV_HIST, cols]
        filler(1)
        act_tiles.append(_silu(acc))
    act = jnp.concatenate(act_tiles, axis=2)

    hp = rest_buf[:, :, POOL_OFF:POOL_OFF + POOL_DIM]
    pext = jnp.concatenate([pool_hist[...], hp], axis=1)
    s2 = pext + pltpu.roll(pext, 1, 1)
    s4 = s2 + pltpu.roll(s2, 2, 1)
    s8 = s4 + pltpu.roll(s4, 4, 1)
    s16 = s8 + pltpu.roll(s8, 8, 1)
    sl = slice(POOL_HIST, POOL_HIST + c)
    lane_p = lax.broadcasted_iota(jnp.int32, (1, c, POOL_DIM), 2)
    sums = jnp.where(lane_p < 64, s2[:, sl], jnp.where(lane_p < 128, s4[:, sl],
                     jnp.where(lane_p < 192, s8[:, sl], s16[:, sl])))
    pos = ((s - 1) * c + 1 + lax.broadcasted_iota(jnp.int32, (1, c, POOL_DIM), 1)).astype(F32)
    win = jnp.where(lane_p < 64, 2.0, jnp.where(lane_p < 128, 4.0, jnp.where(lane_p < 192, 8.0, 16.0)))
    pooled = sums / jnp.maximum(jnp.minimum(pos, win), 1.0) - hp
    y = jnp.dot(pooled.reshape(nb * c, POOL_DIM).astype(BF16), pw_ref[...],
                preferred_element_type=F32) * ps_ref[...]
    mixed_buf[:, :, A_DIM:A_DIM + POOL_DIM] = y.reshape(nb, c, POOL_DIM)
    pool_hist[...] = pext[:, c:c + POOL_HIST, :]
    filler(1)

    mext = jnp.concatenate([m_hist[...], rest_buf[:, :, CC_OFF:CC_OFF + CONV_DIM]
                            * rest_buf[:, :, CH_OFF:CH_OFF + CONV_DIM]], axis=1)
    yc = (mext * sw_ref[2:3, :] + pltpu.roll(mext, 1, 1) * sw_ref[1:2, :]
          + pltpu.roll(mext, 2, 1) * sw_ref[0:1, :])
    mixed_buf[:, :, A_DIM + POOL_DIM:D_MODEL] = (rest_buf[:, :, CB_OFF:CB_OFF + CONV_DIM]
                                                 * yc[:, QKV_HIST:QKV_HIST + c, :])
    m_hist[...] = mext[:, c:c + QKV_HIST, :]
    filler(3)

    ab = rest_buf[:, :, AB_OFF:AB_OFF + AB_PAD]
    lane_ab = lax.broadcasted_iota(jnp.int32, (1, AB_PAD), 1)
    alog = jnp.zeros((1, AB_PAD), F32)
    dtb = jnp.zeros((1, AB_PAD), F32)
    for hd in range(HEADS):
        alog = jnp.where(lane_ab == hd, alog_ref[layer, hd], alog)
        dtb = jnp.where(lane_ab == hd, dtb_ref[layer, hd], dtb)
    g = -jnp.exp(alog) * _softplus(ab + dtb)
    beta_all = _sigmoid(ab)
    row_c = lax.broadcasted_iota(jnp.int32, (1, c, AB_PAD), 1)
    gc = g
    sh = 1
    while sh < c:
        gc = gc + jnp.where(row_c >= sh, pltpu.roll(gc, sh, 1), 0.0)
        sh *= 2

    def heads(f):
        return jnp.concatenate([f(hd) for hd in range(HEADS)], axis=0)

    out_gate = on_ref[...] * _silu(
        heads(lambda hd: rest_buf[:, :, Z_OFF + hd * HEAD_DIM:Z_OFF + (hd + 1) * HEAD_DIM]))
    o, new_state = _delta_chunk(
        heads(lambda hd: act[:, :, hd * HEAD_DIM:(hd + 1) * HEAD_DIM]),
        heads(lambda hd: act[:, :, A_DIM + hd * HEAD_DIM:A_DIM + (hd + 1) * HEAD_DIM]),
        heads(lambda hd: act[:, :, 2 * A_DIM + hd * HEAD_DIM:2 * A_DIM + (hd + 1) * HEAD_DIM]),
        gc,
        heads(lambda hd: beta_all[:, :, HEADS + hd:HEADS + hd + 1]),
        state_ref[...], filler)
    state_ref[...] = new_state
    o = o * lax.rsqrt(jnp.mean(o * o, axis=-1, keepdims=True) + EPS) * out_gate
    for hd in range(HEADS):
        mixed_buf[:, :, hd * HEAD_DIM:(hd + 1) * HEAD_DIM] = o[hd * nb:(hd + 1) * nb]
    filler(D_IN_PAD // A_TILE + 1)

    h_buf[...] = _rms_norm(xn_ref[...].reshape(nb * c, D_MODEL), n1_ref[...]).astype(BF16)

    mixed = mixed_buf[...].reshape(nb * c, D_MODEL).astype(BF16)
    out = xc_ref[...].reshape(nb * c, D_MODEL) + jnp.dot(mixed, wout_ref[...], preferred_element_type=F32)
    o_ref[...] = out.reshape(nb, c, D_MODEL)


def _ffn_kernel(x_ref, p_ref, n2_ref, wg_ref, wu_ref, wd_ref, pg_ref, pp_ref, fg_ref, o_ref, ff_buf,
                *, final_norm):
    x = x_ref[...]
    h = _rms_norm(x, n2_ref[...]).astype(BF16)
    for ci in range(D_FF // FF_CHUNK):
        sl = slice(ci * FF_CHUNK, (ci + 1) * FF_CHUNK)
        gate = jnp.dot(h, wg_ref[:, sl], preferred_element_type=F32)
        up = jnp.dot(h, wu_ref[:, sl], preferred_element_type=F32)
        ff_buf[:, sl] = (_silu(gate) * up).astype(BF16)
    x = x + jnp.dot(ff_buf[...], wd_ref[...], preferred_element_type=F32)
    gate = _sigmoid(jnp.dot(x.astype(BF16), pg_ref[...], preferred_element_type=F32))
    emb = jnp.dot(p_ref[...].astype(BF16), pp_ref[...], preferred_element_type=F32)
    x = x + gate * emb
    if final_norm:
        x = _rms_norm(x, fg_ref[...])
    o_ref[...] = x


def _prep_win_kernel(win_ref, win_o):
    win_o[0:AB_LO, :] = win_ref[0:AB_LO, :].astype(BF16)
    win_o[AB_LO:D_IN_PAD - AB_PAD, :] = win_ref[AB_LO + 2 * HEADS:D_IN, :].astype(BF16)
    win_o[D_IN_PAD - AB_PAD:D_IN_PAD, :] = jnp.concatenate(
        [win_ref[AB_LO:AB_LO + 2 * HEADS, :], jnp.zeros((AB_PAD - 2 * HEADS, D_MODEL), F32)],
        axis=0).astype(BF16)


def _prep_kernel(wout_ref, pw_ref, wout_o, pw_o):
    wout_o[...] = wout_ref[...].astype(BF16)
    r = lax.broadcasted_iota(jnp.int32, (POOL_GROUP_DIM, POOL_DIM), 0)
    l = lax.broadcasted_iota(jnp.int32, (POOL_GROUP_DIM, POOL_DIM), 1)
    spread = jnp.where(l % POOL_GROUP_DIM == r, 1.0, 0.0).astype(BF16)
    tiled = jnp.dot(pw_ref[...].astype(BF16), spread, preferred_element_type=F32)
    rr = lax.broadcasted_iota(jnp.int32, (POOL_DIM, POOL_DIM), 0)
    ll = lax.broadcasted_iota(jnp.int32, (POOL_DIM, POOL_DIM), 1)
    pw_o[...] = jnp.where(rr // POOL_GROUP_DIM == ll // POOL_GROUP_DIM, tiled, 0.0).astype(BF16)


def _prep_call(w_in, w_out, pool_w):
    depth = w_in.shape[0]
    params = pltpu.CompilerParams(dimension_semantics=("arbitrary", "arbitrary"),
                                  vmem_limit_bytes=VMEM_LIMIT_BYTES)
    w_in_t = jnp.swapaxes(w_in, 1, 2)
    win = pl.pallas_call(
        _prep_win_kernel,
        grid=(depth, 1),
        in_specs=[pl.BlockSpec((None, D_IN, D_MODEL), lambda l, r: (l, 0, 0))],
        out_specs=pl.BlockSpec((None, D_IN_PAD, D_MODEL), lambda l, r: (l, 0, 0)),
        out_shape=jax.ShapeDtypeStruct((depth, D_IN_PAD, D_MODEL), BF16),
        compiler_params=params,
        name="prep_win",
    )(w_in_t)

    pool_w = pool_w.reshape(depth, POOL_DIM, POOL_GROUP_DIM)
    wout, pw = pl.pallas_call(
        _prep_kernel,
        grid=(depth, 1),
        in_specs=[pl.BlockSpec((None, D_MODEL, D_MODEL), lambda l, r: (l, 0, 0)),
                  pl.BlockSpec((None, POOL_DIM, POOL_GROUP_DIM), lambda l, r: (l, 0, 0))],
        out_specs=[pl.BlockSpec((None, D_MODEL, D_MODEL), lambda l, r: (l, 0, 0)),
                   pl.BlockSpec((None, POOL_DIM, POOL_DIM), lambda l, r: (l, 0, 0))],
        out_shape=[jax.ShapeDtypeStruct((depth, D_MODEL, D_MODEL), BF16),
                   jax.ShapeDtypeStruct((depth, POOL_DIM, POOL_DIM), BF16)],
        compiler_params=params,
        name="prep",
    )(w_out, pool_w)
    return win, wout, pw


def _layer_spec(a, layer):
    nd = a.ndim - 1
    return pl.BlockSpec((None,) + a.shape[1:], lambda *_: (layer,) + (0,) * nd,
                        pipeline_mode=pl.Buffered(1))


_SMEM_SPEC = pl.BlockSpec(memory_space=pltpu.SMEM)


def _cast_spec(a, layer, n_blocks, axis):
    shape = [None, a.shape[1], a.shape[2]]
    shape[axis] = a.shape[axis] // n_blocks

    def index(s):
        idx = [layer, 0, 0]
        idx[axis] = jnp.minimum(s, n_blocks - 1)
        return tuple(idx)

    return pl.BlockSpec(tuple(shape), index)


def _mixer_call(x, layer, n1, win, cw, alog, dtb, on, pw, ps, sw, wout, ffn_weights):
    nb, seq, _ = x.shape
    c = CHUNK
    n_chunks = seq // c
    nxt = pl.BlockSpec((nb, c, D_MODEL), lambda s: (0, jnp.minimum(s + 1, n_chunks - 1), 0))
    cur = pl.BlockSpec((nb, c, D_MODEL), lambda s: (0, jnp.maximum(s - 1, 0), 0))
    vmem_consts = (n1, win, cw, on, pw, ps, sw, wout)
    w_gate, w_up, w_down, ple_gate, ple_proj = ffn_weights
    cast_plan = ((w_gate, 32, 1), (w_up, 32, 1), (w_down, 8, 2), (ple_gate, 32, 1), (ple_proj, 16, 1))
    cast_specs = [_cast_spec(a, layer, n, ax) for a, n, ax in cast_plan]
    cast_specs_out = [_cast_spec(a, 0, n, ax) for a, n, ax in cast_plan]
    outs = pl.pallas_call(
        functools.partial(_mixer_kernel, layer=layer),
        grid=(n_chunks + 1,),
        in_specs=[nxt, cur, _SMEM_SPEC, _SMEM_SPEC] + [_layer_spec(a, layer) for a in vmem_consts]
                 + cast_specs,
        out_specs=[cur] + cast_specs_out,
        out_shape=[jax.ShapeDtypeStruct(x.shape, F32)]
                  + [jax.ShapeDtypeStruct((1,) + a.shape[1:], BF16) for a, _, _ in cast_plan],
        scratch_shapes=[
            pltpu.VMEM((nb * c, D_MODEL), BF16),
            pltpu.VMEM((nb, QKV_HIST + c, QKV_DIM), F32),
            pltpu.VMEM((nb, c, REST_DIM), F32),
            pltpu.VMEM((nb, POOL_HIST, POOL_DIM), F32),
            pltpu.VMEM((nb, QKV_HIST, CONV_DIM), F32),
            pltpu.VMEM((nb, c, D_MODEL), F32),
            pltpu.VMEM((HEADS * nb, HEAD_DIM, HEAD_DIM), F32),
        ],
        compiler_params=pltpu.CompilerParams(
            dimension_semantics=("arbitrary",), vmem_limit_bytes=VMEM_LIMIT_BYTES),
        name="mixer",
    )(x, x, alog, dtb, *vmem_consts, w_gate, w_up, w_down, ple_gate, ple_proj)
    return outs[0], outs[1:]


def _ffn_call(x2d, p3d, layer, n2, ffn_weights, fg, final_norm):
    tokens = x2d.shape[0]
    return pl.pallas_call(
        functools.partial(_ffn_kernel, final_norm=final_norm),
        grid=(tokens // FFN_TILE,),
        in_specs=[pl.BlockSpec((FFN_TILE, D_MODEL), lambda i: (i, 0)),
                  pl.BlockSpec((None, FFN_TILE, PLE_DIM), lambda i: (layer, i, 0))]
                 + [_layer_spec(n2, layer)] + [_layer_spec(a, 0) for a in ffn_weights]
                 + [pl.BlockSpec(fg.shape, lambda i: (0, 0), pipeline_mode=pl.Buffered(1))],
        out_specs=pl.BlockSpec((FFN_TILE, D_MODEL), lambda i: (i, 0)),
        out_shape=jax.ShapeDtypeStruct(x2d.shape, F32),
        scratch_shapes=[pltpu.VMEM((FFN_TILE, D_FF), BF16)],
        compiler_params=pltpu.CompilerParams(
            dimension_semantics=("arbitrary",), vmem_limit_bytes=VMEM_LIMIT_BYTES),
        name="ffn",
    )(x2d, p3d, n2, *ffn_weights, fg)


def kernel(x, p, norm1_g, w_in, conv_qkv, a_log, dt_bias, onorm_g, pool_w, pool_scale, sconv_w, w_out,
           norm2_g, w_gate, w_up, w_down, ple_proj, ple_gate, final_g):
    depth = w_in.shape[0]
    nb, seq, _ = x.shape
    win, wout, pw = _prep_call(w_in, w_out, pool_w)
    n1, on, ps, n2 = (a.reshape(depth, 1, a.shape[-1]) for a in (norm1_g, onorm_g, pool_scale, norm2_g))
    p3d = p.reshape(depth, nb * seq, PLE_DIM)
    fg = final_g.reshape(1, D_MODEL)
    for i in range(depth):
        x, ffn_weights = _mixer_call(x, i, n1, win, conv_qkv, a_log, dt_bias, on, pw, ps, sconv_w, wout,
                                     (w_gate, w_up, w_down, ple_gate, ple_proj))
        x = _ffn_call(x.reshape(nb * seq, D_MODEL), p3d, i, n2, ffn_weights, fg,
                      final_norm=(i == depth - 1)).reshape(nb, seq, D_MODEL)
    return x
```

```python
import functools

import jax
import jax.numpy as jnp
from jax import lax
from jax.experimental import pallas as pl
from jax.experimental.pallas import tpu as pltpu

D_MODEL = 1024
PLE_DIM = 256
EPS = 1e-6
HEAD_DIM = 128
HEADS = 4
A_DIM = HEADS * HEAD_DIM
QKV_DIM = 3 * A_DIM
QKV_CONV_WIDTH = 4
CHUNK = 64
POOL_WINDOWS = (2, 4, 8, 16)
POOL_DIM = 256
POOL_GROUP_DIM = 64
CONV_DIM = 256
CONV_WIDTH = 3
D_FF = 2816
AB_PAD = 128
REST_DIM = A_DIM + POOL_DIM + 3 * CONV_DIM + AB_PAD
Z_OFF, POOL_OFF, CB_OFF, CC_OFF, CH_OFF, AB_OFF = 0, 512, 768, 1024, 1280, 1536
D_IN_PAD = QKV_DIM + REST_DIM
AB_LO = 4 * A_DIM
D_IN = AB_LO + 2 * HEADS + POOL_DIM + 3 * CONV_DIM

QKV_HIST = 8
POOL_HIST = 16
VMEM_LIMIT_BYTES = 56 * 1024 * 1024
FFN_TILE = 1024
FF_CHUNK = 256
A_TILE = 256

BF16 = jnp.bfloat16
F32 = jnp.float32


def _sigmoid(x):
    return 0.5 * jnp.tanh(0.5 * x) + 0.5


def _silu(x):
    hx = 0.5 * x
    return hx + hx * jnp.tanh(hx)


def _softplus(x):
    return jnp.maximum(x, 0.0) + jnp.log1p(jnp.exp(-jnp.abs(x)))


def _rms_scale(x):
    return x * lax.rsqrt(jnp.mean(x * x, axis=-1, keepdims=True) + EPS)


def _rms_norm(x, g):
    return _rms_scale(x) * g


def _bmm(a, b):
    return lax.dot_general(a.astype(BF16), b.astype(BF16), (((2,), (1,)), ((0,), (0,))),
                           preferred_element_type=F32)


def _bmm_nt(a, b):
    return lax.dot_general(a.astype(BF16), b.astype(BF16), (((2,), (2,)), ((0,), (0,))),
                           preferred_element_type=F32)


def _delta_chunk(qr, kr, v, gcol, grow, beta, state, filler):
    c = CHUNK
    rq = lax.rsqrt(jnp.sum(qr * qr, axis=-1, keepdims=True) + EPS) * (HEAD_DIM ** -0.5)
    rk = lax.rsqrt(jnp.sum(kr * kr, axis=-1, keepdims=True) + EPS)
    q = qr * rq
    k = kr * rk
    eg = jnp.exp(gcol)
    g_last = gcol[:, c - 1:c, :]
    kb = k * beta
    vb = v * beta
    row = lax.broadcasted_iota(jnp.int32, (1, c, c), 1)
    col = lax.broadcasted_iota(jnp.int32, (1, c, c), 2)
    decay = jnp.exp(jnp.where(row >= col, gcol - grow, -1e30))
    kq = _bmm_nt(jnp.concatenate([kb, q], axis=1), k)
    lower = jnp.where(row > col, kq[:, 0:c] * decay, 0.0)
    attn = kq[:, c:2 * c] * decay
    filler(1)
    x = -lower
    t = jnp.where(row == col, 1.0, 0.0) + x
    x = _bmm(x, x)
    n_sq = CHUNK.bit_length() - 2
    for _ in range(n_sq - 1):
        tx = _bmm(jnp.concatenate([x, t], axis=1), x)
        x = tx[:, 0:c]
        t = t + tx[:, c:2 * c]
        filler(1)
    t = t + _bmm(t, x)
    uw = _bmm(t, jnp.concatenate([vb, kb * eg], axis=2))
    ws = _bmm(jnp.concatenate([uw[:, :, HEAD_DIM:2 * HEAD_DIM], q * eg], axis=1), state)
    v_new = uw[:, :, 0:HEAD_DIM] - ws[:, 0:c]
    o = ws[:, c:2 * c] + _bmm(attn, v_new)
    kd = k * jnp.exp(g_last - gcol)
    new_state = state * jnp.exp(g_last) + _bmm(jnp.swapaxes(kd, 1, 2), v_new)
    return o, new_state


def _mixer_kernel(xn_ref, xc_ref, alog_ref, dtb_ref, win_ref, cw_ref, on_ref, pw_ref, ps_ref,
                  sw_ref, wout_ref, wg_ref, wu_ref, wd_ref, pg_ref, pp_ref,
                  o_ref, wg_o, wu_o, wd_o, pg_o, pp_o,
                  h_buf, qkv_ext, rest_buf, pool_hist, m_hist, mixed_buf, state_ref, *, layer):
    s = pl.program_id(0)
    nb = xn_ref.shape[0]
    c = CHUNK

    for src_ref, dst_ref in ((wg_ref, wg_o), (wu_ref, wu_o), (wd_ref, wd_o), (pg_ref, pg_o), (pp_ref, pp_o)):
        dst_ref[...] = src_ref[...].astype(BF16)

    @pl.when(s == 0)
    def _():
        qkv_ext[...] = jnp.zeros(qkv_ext.shape, F32)
        rest_buf[...] = jnp.zeros(rest_buf.shape, F32)
        pool_hist[...] = jnp.zeros(pool_hist.shape, F32)
        m_hist[...] = jnp.zeros(m_hist.shape, F32)
        state_ref[...] = jnp.zeros(state_ref.shape, F32)
        h_buf[...] = _rms_scale(xc_ref[...].reshape(nb * c, D_MODEL)).astype(BF16)

    rest_tiles = [QKV_DIM + off for off in (POOL_OFF, CB_OFF, CC_OFF, CH_OFF, AB_OFF, Z_OFF, Z_OFF + A_TILE)]
    a_tiles = iter(list(range(0, QKV_DIM, A_TILE)) + rest_tiles)

    def filler(n):
        for _ in range(n):
            lo = next(a_tiles, None)
            if lo is None:
                return
            hi = min(lo + A_TILE, D_IN_PAD)
            tile = lax.dot_general(h_buf[...], win_ref[lo:hi, :], (((1,), (1,)), ((), ())),
                                   preferred_element_type=F32).reshape(nb, c, hi - lo)
            if hi <= QKV_DIM:
                qkv_ext[:, QKV_HIST:QKV_HIST + c, lo:hi] = tile
            else:
                rest_buf[:, :, lo - QKV_DIM:hi - QKV_DIM] = tile


    act_tiles = []
    for lo_c in range(0, QKV_DIM, A_TILE):
        cols = slice(lo_c, lo_c + A_TILE)
        ext = qkv_ext[:, :, cols]
        acc = ext * cw_ref[QKV_CONV_WIDTH - 1:QKV_CONV_WIDTH, cols]
        for j in range(QKV_CONV_WIDTH - 1):
            acc = acc + pltpu.roll(ext, QKV_CONV_WIDTH - 1 - j, 1) * cw_ref[j:j + 1, cols]
        qkv_ext[:, 0:QKV_HIST, cols] = ext[:, c:c + QKV_HIST, :]
        filler(1)
        act_tiles.append(_silu(acc[:, QKV_HIST:QKV_HIST + c, :]))
    act = jnp.concatenate(act_tiles, axis=2)

    hp = rest_buf[:, :, POOL_OFF:POOL_OFF + POOL_DIM]
    pext = jnp.concatenate([pool_hist[...], hp], axis=1)
    s2 = pext + pltpu.roll(pext, 1, 1)
    s4 = s2 + pltpu.roll(s2, 2, 1)
    s8 = s4 + pltpu.roll(s4, 4, 1)
    s16 = s8 + pltpu.roll(s8, 8, 1)
    sl = slice(POOL_HIST, POOL_HIST + c)
    lane_p = lax.broadcasted_iota(jnp.int32, (1, c, POOL_DIM), 2)
    sums = jnp.where(lane_p < 64, s2[:, sl], jnp.where(lane_p < 128, s4[:, sl],
                     jnp.where(lane_p < 192, s8[:, sl], s16[:, sl])))
    pos = ((s - 1) * c + 1 + lax.broadcasted_iota(jnp.int32, (1, c, POOL_DIM), 1)).astype(F32)
    win = jnp.where(lane_p < 64, 2.0, jnp.where(lane_p < 128, 4.0, jnp.where(lane_p < 192, 8.0, 16.0)))
    pooled = sums / jnp.maximum(jnp.minimum(pos, win), 1.0) - hp
    y = jnp.dot(pooled.reshape(nb * c, POOL_DIM).astype(BF16), pw_ref[...],
                preferred_element_type=F32) * ps_ref[...]
    mixed_buf[:, :, A_DIM:A_DIM + POOL_DIM] = y.reshape(nb, c, POOL_DIM)
    pool_hist[...] = pext[:, c:c + POOL_HIST, :]
    filler(1)

    mext = jnp.concatenate([m_hist[...], rest_buf[:, :, CC_OFF:CC_OFF + CONV_DIM]
                            * rest_buf[:, :, CH_OFF:CH_OFF + CONV_DIM]], axis=1)
    yc = (mext * sw_ref[2:3, :] + pltpu.roll(mext, 1, 1) * sw_ref[1:2, :]
          + pltpu.roll(mext, 2, 1) * sw_ref[0:1, :])
    mixed_buf[:, :, A_DIM + POOL_DIM:D_MODEL] = (rest_buf[:, :, CB_OFF:CB_OFF + CONV_DIM]
                                                 * yc[:, QKV_HIST:QKV_HIST + c, :])
    m_hist[...] = mext[:, c:c + QKV_HIST, :]
    filler(3)

    ab = rest_buf[:, :, AB_OFF:AB_OFF + AB_PAD]
    filler(1)
    lane_ab = lax.broadcasted_iota(jnp.int32, (1, AB_PAD), 1)
    alog = jnp.zeros((1, AB_PAD), F32)
    dtb = jnp.zeros((1, AB_PAD), F32)
    for hd in range(HEADS):
        alog = jnp.where(lane_ab == hd, alog_ref[layer, hd], alog)
        dtb = jnp.where(lane_ab == hd, dtb_ref[layer, hd], dtb)
    g = -jnp.exp(alog) * _softplus(ab + dtb)
    beta_all = _sigmoid(ab)
    row_c = lax.broadcasted_iota(jnp.int32, (1, c, AB_PAD), 1)
    gc = g
    sh = 1
    while sh < c:
        gc = gc + jnp.where(row_c >= sh, pltpu.roll(gc, sh, 1), 0.0)
        sh *= 2
    gct = jnp.swapaxes(gc, 1, 2)

    def heads(f):
        return jnp.concatenate([f(hd) for hd in range(HEADS)], axis=0)

    out_gate = on_ref[...] * _silu(
        heads(lambda hd: rest_buf[:, :, Z_OFF + hd * HEAD_DIM:Z_OFF + (hd + 1) * HEAD_DIM]))
    filler(2)
    o, new_state = _delta_chunk(
        heads(lambda hd: act[:, :, hd * HEAD_DIM:(hd + 1) * HEAD_DIM]),
        heads(lambda hd: act[:, :, A_DIM + hd * HEAD_DIM:A_DIM + (hd + 1) * HEAD_DIM]),
        heads(lambda hd: act[:, :, 2 * A_DIM + hd * HEAD_DIM:2 * A_DIM + (hd + 1) * HEAD_DIM]),
        heads(lambda hd: gc[:, :, hd:hd + 1]),
        heads(lambda hd: gct[:, hd:hd + 1, :]),
        heads(lambda hd: beta_all[:, :, HEADS + hd:HEADS + hd + 1]),
        state_ref[...], filler)
    state_ref[...] = new_state
    o = o * lax.rsqrt(jnp.mean(o * o, axis=-1, keepdims=True) + EPS) * out_gate
    for hd in range(HEADS):
        mixed_buf[:, :, hd * HEAD_DIM:(hd + 1) * HEAD_DIM] = o[hd * nb:(hd + 1) * nb]
    filler(D_IN_PAD // A_TILE + 1)

    h_buf[...] = _rms_scale(xn_ref[...].reshape(nb * c, D_MODEL)).astype(BF16)

    mixed = mixed_buf[...].reshape(nb * c, D_MODEL).astype(BF16)
    out = xc_ref[...].reshape(nb * c, D_MODEL) + jnp.dot(mixed, wout_ref[...], preferred_element_type=F32)
    o_ref[...] = out.reshape(nb, c, D_MODEL)


def _ffn_kernel(x_ref, p_ref, n2_ref, wg_ref, wu_ref, wd_ref, pg_ref, pp_ref, fg_ref, o_ref, ff_buf,
                *, final_norm):
    x = x_ref[...]
    h = _rms_norm(x, n2_ref[...]).astype(BF16)
    for ci in range(D_FF // FF_CHUNK):
        sl = slice(ci * FF_CHUNK, (ci + 1) * FF_CHUNK)
        gate = jnp.dot(h, wg_ref[:, sl], preferred_element_type=F32)
        up = jnp.dot(h, wu_ref[:, sl], preferred_element_type=F32)
        ff_buf[:, sl] = (_silu(gate) * up).astype(BF16)
    x = x + jnp.dot(ff_buf[...], wd_ref[...], preferred_element_type=F32)
    gate = _sigmoid(jnp.dot(x.astype(BF16), pg_ref[...], preferred_element_type=F32))
    emb = jnp.dot(p_ref[...].astype(BF16), pp_ref[...], preferred_element_type=F32)
    x = x + gate * emb
    if final_norm:
        x = _rms_norm(x, fg_ref[...])
    o_ref[...] = x


def _prep_win_kernel(win_ref, g_ref, win_o):
    g = g_ref[...]
    win_o[0:AB_LO, :] = (win_ref[0:AB_LO, :] * g).astype(BF16)
    win_o[AB_LO:D_IN_PAD - AB_PAD, :] = (win_ref[AB_LO + 2 * HEADS:D_IN, :] * g).astype(BF16)
    win_o[D_IN_PAD - AB_PAD:D_IN_PAD, :] = jnp.concatenate(
        [win_ref[AB_LO:AB_LO + 2 * HEADS, :] * g, jnp.zeros((AB_PAD - 2 * HEADS, D_MODEL), F32)],
        axis=0).astype(BF16)


def _prep_kernel(wout_ref, pw_ref, wout_o, pw_o):
    wout_o[...] = wout_ref[...].astype(BF16)
    r = lax.broadcasted_iota(jnp.int32, (POOL_GROUP_DIM, POOL_DIM), 0)
    l = lax.broadcasted_iota(jnp.int32, (POOL_GROUP_DIM, POOL_DIM), 1)
    spread = jnp.where(l % POOL_GROUP_DIM == r, 1.0, 0.0).astype(BF16)
    tiled = jnp.dot(pw_ref[...].astype(BF16), spread, preferred_element_type=F32)
    rr = lax.broadcasted_iota(jnp.int32, (POOL_DIM, POOL_DIM), 0)
    ll = lax.broadcasted_iota(jnp.int32, (POOL_DIM, POOL_DIM), 1)
    pw_o[...] = jnp.where(rr // POOL_GROUP_DIM == ll // POOL_GROUP_DIM, tiled, 0.0).astype(BF16)


def _prep_call(w_in, norm1_g, w_out, pool_w):
    depth = w_in.shape[0]
    params = pltpu.CompilerParams(dimension_semantics=("arbitrary", "arbitrary"),
                                  vmem_limit_bytes=VMEM_LIMIT_BYTES)
    w_in_t = jnp.swapaxes(w_in, 1, 2)
    win = pl.pallas_call(
        _prep_win_kernel,
        grid=(depth, 1),
        in_specs=[pl.BlockSpec((None, D_IN, D_MODEL), lambda l, r: (l, 0, 0)),
                  pl.BlockSpec((None, 1, D_MODEL), lambda l, r: (l, 0, 0))],
        out_specs=pl.BlockSpec((None, D_IN_PAD, D_MODEL), lambda l, r: (l, 0, 0)),
        out_shape=jax.ShapeDtypeStruct((depth, D_IN_PAD, D_MODEL), BF16),
        compiler_params=params,
        name="prep_win",
    )(w_in_t, norm1_g)

    pool_w = pool_w.reshape(depth, POOL_DIM, POOL_GROUP_DIM)
    wout, pw = pl.pallas_call(
        _prep_kernel,
        grid=(depth, 1),
        in_specs=[pl.BlockSpec((None, D_MODEL, D_MODEL), lambda l, r: (l, 0, 0)),
                  pl.BlockSpec((None, POOL_DIM, POOL_GROUP_DIM), lambda l, r: (l, 0, 0))],
        out_specs=[pl.BlockSpec((None, D_MODEL, D_MODEL), lambda l, r: (l, 0, 0)),
                   pl.BlockSpec((None, POOL_DIM, POOL_DIM), lambda l, r: (l, 0, 0))],
        out_shape=[jax.ShapeDtypeStruct((depth, D_MODEL, D_MODEL), BF16),
                   jax.ShapeDtypeStruct((depth, POOL_DIM, POOL_DIM), BF16)],
        compiler_params=params,
        name="prep",
    )(w_out, pool_w)
    return win, wout, pw


def _layer_spec(a, layer):
    nd = a.ndim - 1
    return pl.BlockSpec((None,) + a.shape[1:], lambda *_: (layer,) + (0,) * nd,
                        pipeline_mode=pl.Buffered(1))


_SMEM_SPEC = pl.BlockSpec(memory_space=pltpu.SMEM)


def _cast_spec(a, layer, n_blocks, axis):
    shape = [None, a.shape[1], a.shape[2]]
    shape[axis] = a.shape[axis] // n_blocks

    def index(s):
        idx = [layer, 0, 0]
        idx[axis] = jnp.minimum(s, n_blocks - 1)
        return tuple(idx)

    return pl.BlockSpec(tuple(shape), index)


def _mixer_call(x, layer, win, cw, alog, dtb, on, pw, ps, sw, wout, ffn_weights):
    nb, seq, _ = x.shape
    c = CHUNK
    n_chunks = seq // c
    nxt = pl.BlockSpec((nb, c, D_MODEL), lambda s: (0, jnp.minimum(s + 1, n_chunks - 1), 0))
    cur = pl.BlockSpec((nb, c, D_MODEL), lambda s: (0, jnp.maximum(s - 1, 0), 0))
    vmem_consts = (win, cw, on, pw, ps, sw, wout)
    w_gate, w_up, w_down, ple_gate, ple_proj = ffn_weights
    cast_plan = ((w_gate, 32, 1), (w_up, 32, 1), (w_down, 8, 2), (ple_gate, 32, 1), (ple_proj, 16, 1))
    cast_specs = [_cast_spec(a, layer, n, ax) for a, n, ax in cast_plan]
    cast_specs_out = [_cast_spec(a, 0, n, ax) for a, n, ax in cast_plan]
    outs = pl.pallas_call(
        functools.partial(_mixer_kernel, layer=layer),
        grid=(n_chunks + 1,),
        in_specs=[nxt, cur, _SMEM_SPEC, _SMEM_SPEC] + [_layer_spec(a, layer) for a in vmem_consts]
                 + cast_specs,
        out_specs=[cur] + cast_specs_out,
        out_shape=[jax.ShapeDtypeStruct(x.shape, F32)]
                  + [jax.ShapeDtypeStruct((1,) + a.shape[1:], BF16) for a, _, _ in cast_plan],
        scratch_shapes=[
            pltpu.VMEM((nb * c, D_MODEL), BF16),
            pltpu.VMEM((nb, QKV_HIST + c, QKV_DIM), F32),
            pltpu.VMEM((nb, c, REST_DIM), F32),
            pltpu.VMEM((nb, POOL_HIST, POOL_DIM), F32),
            pltpu.VMEM((nb, QKV_HIST, CONV_DIM), F32),
            pltpu.VMEM((nb, c, D_MODEL), F32),
            pltpu.VMEM((HEADS * nb, HEAD_DIM, HEAD_DIM), F32),
        ],
        compiler_params=pltpu.CompilerParams(
            dimension_semantics=("arbitrary",), vmem_limit_bytes=VMEM_LIMIT_BYTES),
        name="mixer",
    )(x, x, alog, dtb, *vmem_consts, w_gate, w_up, w_down, ple_gate, ple_proj)
    return outs[0], outs[1:]


def _ffn_call(x2d, p3d, layer, n2, ffn_weights, fg, final_norm):
    tokens = x2d.shape[0]
    return pl.pallas_call(
        functools.partial(_ffn_kernel, final_norm=final_norm),
        grid=(tokens // FFN_TILE,),
        in_specs=[pl.BlockSpec((FFN_TILE, D_MODEL), lambda i: (i, 0)),
                  pl.BlockSpec((None, FFN_TILE, PLE_DIM), lambda i: (layer, i, 0))]
                 + [_layer_spec(n2, layer)] + [_layer_spec(a, 0) for a in ffn_weights]
                 + [pl.BlockSpec(fg.shape, lambda i: (0, 0), pipeline_mode=pl.Buffered(1))],
        out_specs=pl.BlockSpec((FFN_TILE, D_MODEL), lambda i: (i, 0)),
        out_shape=jax.ShapeDtypeStruct(x2d.shape, F32),
        scratch_shapes=[pltpu.VMEM((FFN_TILE, D_FF), BF16)],
        compiler_params=pltpu.CompilerParams(
            dimension_semantics=("arbitrary",), vmem_limit_bytes=VMEM_LIMIT_BYTES),
        name="ffn",
    )(x2d, p3d, n2, *ffn_weights, fg)


def kernel(x, p, norm1_g, w_in, conv_qkv, a_log, dt_bias, onorm_g, pool_w, pool_scale, sconv_w, w_out,
           norm2_g, w_gate, w_up, w_down, ple_proj, ple_gate, final_g):
    depth = w_in.shape[0]
    nb, seq, _ = x.shape
    n1, on, ps, n2 = (a.reshape(depth, 1, a.shape[-1]) for a in (norm1_g, onorm_g, pool_scale, norm2_g))
    win, wout, pw = _prep_call(w_in, n1, w_out, pool_w)
    p3d = p.reshape(depth, nb * seq, PLE_DIM)
    fg = final_g.reshape(1, D_MODEL)
    for i in range(depth):
        x, ffn_weights = _mixer_call(x, i, win, conv_qkv, a_log, dt_bias, on, pw, ps, sconv_w, wout,
                                     (w_gate, w_up, w_down, ple_gate, ple_proj))
        x = _ffn_call(x.reshape(nb * seq, D_MODEL), p3d, i, n2, ffn_weights, fg,
                      final_norm=(i == depth - 1)).reshape(nb, seq, D_MODEL)
    return x
```

```python
import functools

import jax
import jax.numpy as jnp
from jax import lax
from jax.experimental import pallas as pl
from jax.experimental.pallas import tpu as pltpu

D_MODEL = 1024
PLE_DIM = 256
EPS = 1e-6
HEAD_DIM = 128
HEADS = 4
A_DIM = HEADS * HEAD_DIM
QKV_DIM = 3 * A_DIM
QKV_CONV_WIDTH = 4
CHUNK = 64
POOL_WINDOWS = (2, 4, 8, 16)
POOL_DIM = 256
POOL_GROUP_DIM = 64
CONV_DIM = 256
CONV_WIDTH = 3
D_FF = 2816
AB_PAD = 128
REST_DIM = A_DIM + POOL_DIM + 3 * CONV_DIM + AB_PAD
Z_OFF, POOL_OFF, CB_OFF, CC_OFF, CH_OFF, AB_OFF = 0, 512, 768, 1024, 1280, 1536
D_IN_PAD = QKV_DIM + REST_DIM
AB_LO = 4 * A_DIM
D_IN = AB_LO + 2 * HEADS + POOL_DIM + 3 * CONV_DIM

QKV_HIST = 8
POOL_HIST = 16
VMEM_LIMIT_BYTES = 56 * 1024 * 1024
FFN_TILE = 1024
FF_CHUNK = 256
A_TILE = 256

BF16 = jnp.bfloat16
F32 = jnp.float32


def _sigmoid(x):
    return 0.5 * jnp.tanh(0.5 * x) + 0.5


def _silu(x):
    hx = 0.5 * x
    return hx + hx * jnp.tanh(hx)


def _silu_of_double(hx):
    return hx + hx * jnp.tanh(hx)


def _softplus(x):
    return jnp.maximum(x, 0.0) + jnp.log1p(jnp.exp(-jnp.abs(x)))


def _rms_scale(x):
    return x * lax.rsqrt(jnp.mean(x * x, axis=-1, keepdims=True) + EPS)


def _rms_norm(x, g):
    return _rms_scale(x) * g


def _bmm(a, b):
    return lax.dot_general(a.astype(BF16), b.astype(BF16), (((2,), (1,)), ((0,), (0,))),
                           preferred_element_type=F32)


def _bmm_nt(a, b):
    return lax.dot_general(a.astype(BF16), b.astype(BF16), (((2,), (2,)), ((0,), (0,))),
                           preferred_element_type=F32)


def _delta_chunk(qr, kr, v, gcol, grow, beta, state, filler):
    c = CHUNK
    rq = lax.rsqrt(jnp.sum(qr * qr, axis=-1, keepdims=True) + EPS) * (HEAD_DIM ** -0.5)
    rk = lax.rsqrt(jnp.sum(kr * kr, axis=-1, keepdims=True) + EPS)
    q = qr * rq
    k = kr * rk
    eg = jnp.exp(gcol)
    g_last = gcol[:, c - 1:c, :]
    kb = k * beta
    vb = v * beta
    row = lax.broadcasted_iota(jnp.int32, (1, c, c), 1)
    col = lax.broadcasted_iota(jnp.int32, (1, c, c), 2)
    decay = jnp.exp(jnp.where(row >= col, gcol - grow, -1e30))
    kq = _bmm_nt(jnp.concatenate([kb, q], axis=1), k)
    lower = jnp.where(row > col, kq[:, 0:c] * decay, 0.0)
    attn = kq[:, c:2 * c] * decay
    filler(1)
    x = -lower
    t = jnp.where(row == col, 1.0, 0.0) + x
    x = _bmm(x, x)
    n_sq = CHUNK.bit_length() - 2
    for _ in range(n_sq - 1):
        tx = _bmm(jnp.concatenate([x, t], axis=1), x)
        x = tx[:, 0:c]
        t = t + tx[:, c:2 * c]
        filler(1)
    t = t + _bmm(t, x)
    uw = _bmm(t, jnp.concatenate([vb, kb * eg], axis=2))
    ws = _bmm(jnp.concatenate([uw[:, :, HEAD_DIM:2 * HEAD_DIM], q * eg], axis=1), state)
    v_new = uw[:, :, 0:HEAD_DIM] - ws[:, 0:c]
    o = ws[:, c:2 * c] + _bmm(attn, v_new)
    kd = k * jnp.exp(g_last - gcol)
    new_state = state * jnp.exp(g_last) + _bmm(jnp.swapaxes(kd, 1, 2), v_new)
    return o, new_state


def _project_tile(h_buf, win_ref, qkv_ext, rest_buf, lo, nb):
    c = CHUNK
    hi = min(lo + A_TILE, D_IN_PAD)
    tile = lax.dot_general(h_buf[...], win_ref[lo:hi, :], (((1,), (1,)), ((), ())),
                           preferred_element_type=F32).reshape(nb, c, hi - lo)
    if hi <= QKV_DIM:
        qkv_ext[:, QKV_HIST:QKV_HIST + c, lo:hi] = tile
    else:
        rest_buf[:, :, lo - QKV_DIM:hi - QKV_DIM] = tile


def _mixer_kernel(xn_ref, xc_ref, alog_ref, dtb_ref, win_ref, cw_ref, on_ref, pw_ref,
                  sw_ref, wout_ref, wg_ref, wu_ref, wd_ref, pg_ref, pp_ref,
                  o_ref, wg_o, wu_o, wd_o, pg_o, pp_o,
                  h_buf, qkv_ext, rest_buf, pool_hist, m_hist, mixed_buf, state_ref, *, layer):
    s = pl.program_id(0)
    nb = xn_ref.shape[0]
    c = CHUNK

    for src_ref, dst_ref in ((wg_ref, wg_o), (wu_ref, wu_o), (wd_ref, wd_o), (pg_ref, pg_o), (pp_ref, pp_o)):
        dst_ref[...] = src_ref[...].astype(BF16)

    @pl.when(s == 0)
    def _():
        qkv_ext[:, 0:QKV_HIST, :] = jnp.zeros((nb, QKV_HIST, QKV_DIM), F32)
        pool_hist[...] = jnp.zeros(pool_hist.shape, F32)
        m_hist[...] = jnp.zeros(m_hist.shape, F32)
        state_ref[...] = jnp.zeros(state_ref.shape, F32)
        o_ref[...] = xc_ref[...]
        h_buf[...] = _rms_scale(xc_ref[...].reshape(nb * c, D_MODEL)).astype(BF16)
        for lo in range(0, D_IN_PAD, A_TILE):
            _project_tile(h_buf, win_ref, qkv_ext, rest_buf, lo, nb)
        h_buf[...] = _rms_scale(xn_ref[...].reshape(nb * c, D_MODEL)).astype(BF16)

    @pl.when(s > 0)
    def _():
        _mixer_step(s, xn_ref, xc_ref, alog_ref, dtb_ref, win_ref, cw_ref, on_ref, pw_ref,
                    sw_ref, wout_ref, o_ref,
                    h_buf, qkv_ext, rest_buf, pool_hist, m_hist, mixed_buf, state_ref, layer)


def _mixer_step(s, xn_ref, xc_ref, alog_ref, dtb_ref, win_ref, cw_ref, on_ref, pw_ref,
                sw_ref, wout_ref, o_ref,
                h_buf, qkv_ext, rest_buf, pool_hist, m_hist, mixed_buf, state_ref, layer):
    nb = xn_ref.shape[0]
    c = CHUNK

    rest_tiles = [QKV_DIM + off for off in (POOL_OFF, CB_OFF, CC_OFF, CH_OFF, AB_OFF, Z_OFF, Z_OFF + A_TILE)]
    a_tiles = iter(list(range(0, QKV_DIM, A_TILE)) + rest_tiles)

    def filler(n):
        for _ in range(n):
            lo = next(a_tiles, None)
            if lo is None:
                return
            _project_tile(h_buf, win_ref, qkv_ext, rest_buf, lo, nb)


    act_tiles = []
    for lo_c in range(0, QKV_DIM, A_TILE):
        cols = slice(lo_c, lo_c + A_TILE)
        ext = qkv_ext[:, :, cols]
        taps = 0.5 * cw_ref[:, cols]
        acc = ext * taps[QKV_CONV_WIDTH - 1:QKV_CONV_WIDTH]
        for j in range(QKV_CONV_WIDTH - 1):
            acc = acc + pltpu.roll(ext, QKV_CONV_WIDTH - 1 - j, 1) * taps[j:j + 1]
        qkv_ext[:, 0:QKV_HIST, cols] = ext[:, c:c + QKV_HIST, :]
        filler(1)
        act_tiles.append(_silu_of_double(acc[:, QKV_HIST:QKV_HIST + c, :]))
    act = jnp.concatenate(act_tiles, axis=2)

    hp = rest_buf[:, :, POOL_OFF:POOL_OFF + POOL_DIM]
    pext = jnp.concatenate([pool_hist[...], hp], axis=1)
    s2 = pext + pltpu.roll(pext, 1, 1)
    s4 = s2 + pltpu.roll(s2, 2, 1)
    s8 = s4 + pltpu.roll(s4, 4, 1)
    s16 = s8 + pltpu.roll(s8, 8, 1)
    sl = slice(POOL_HIST, POOL_HIST + c)
    lane_p = lax.broadcasted_iota(jnp.int32, (1, c, POOL_DIM), 2)
    sums = jnp.where(lane_p < 64, s2[:, sl], jnp.where(lane_p < 128, s4[:, sl],
                     jnp.where(lane_p < 192, s8[:, sl], s16[:, sl])))
    pos = ((s - 1) * c + 1 + lax.broadcasted_iota(jnp.int32, (1, c, POOL_DIM), 1)).astype(F32)
    win = jnp.where(lane_p < 64, 2.0, jnp.where(lane_p < 128, 4.0, jnp.where(lane_p < 192, 8.0, 16.0)))
    pooled = sums * (1.0 / jnp.minimum(pos, win)) - hp
    y = jnp.dot(pooled.reshape(nb * c, POOL_DIM).astype(BF16), pw_ref[...], preferred_element_type=F32)
    mixed_buf[:, :, A_DIM:A_DIM + POOL_DIM] = y.reshape(nb, c, POOL_DIM)
    pool_hist[...] = pext[:, c:c + POOL_HIST, :]
    filler(1)

    mext = jnp.concatenate([m_hist[...], rest_buf[:, :, CC_OFF:CC_OFF + CONV_DIM]
                            * rest_buf[:, :, CH_OFF:CH_OFF + CONV_DIM]], axis=1)
    yc = (mext * sw_ref[2:3, :] + pltpu.roll(mext, 1, 1) * sw_ref[1:2, :]
          + pltpu.roll(mext, 2, 1) * sw_ref[0:1, :])
    mixed_buf[:, :, A_DIM + POOL_DIM:D_MODEL] = (rest_buf[:, :, CB_OFF:CB_OFF + CONV_DIM]
                                                 * yc[:, QKV_HIST:QKV_HIST + c, :])
    m_hist[...] = mext[:, c:c + QKV_HIST, :]
    filler(3)

    ab = rest_buf[:, :, AB_OFF:AB_OFF + AB_PAD]
    filler(1)
    lane_ab = lax.broadcasted_iota(jnp.int32, (1, AB_PAD), 1)
    alog = jnp.zeros((1, AB_PAD), F32)
    dtb = jnp.zeros((1, AB_PAD), F32)
    for hd in range(HEADS):
        alog = jnp.where(lane_ab == hd, alog_ref[layer, hd], alog)
        dtb = jnp.where(lane_ab == hd, dtb_ref[layer, hd], dtb)
    g = -jnp.exp(alog) * _softplus(ab + dtb)
    beta_all = _sigmoid(ab)
    row_c = lax.broadcasted_iota(jnp.int32, (1, c, AB_PAD), 1)
    gc = g
    sh = 1
    while sh < c:
        gc = gc + jnp.where(row_c >= sh, pltpu.roll(gc, sh, 1), 0.0)
        sh *= 2
    gct = jnp.swapaxes(gc, 1, 2)

    def heads(f):
        return jnp.concatenate([f(hd) for hd in range(HEADS)], axis=0)

    out_gate = on_ref[...] * _silu_of_double(
        heads(lambda hd: rest_buf[:, :, Z_OFF + hd * HEAD_DIM:Z_OFF + (hd + 1) * HEAD_DIM]))
    filler(2)
    o, new_state = _delta_chunk(
        heads(lambda hd: act[:, :, hd * HEAD_DIM:(hd + 1) * HEAD_DIM]),
        heads(lambda hd: act[:, :, A_DIM + hd * HEAD_DIM:A_DIM + (hd + 1) * HEAD_DIM]),
        heads(lambda hd: act[:, :, 2 * A_DIM + hd * HEAD_DIM:2 * A_DIM + (hd + 1) * HEAD_DIM]),
        heads(lambda hd: gc[:, :, hd:hd + 1]),
        heads(lambda hd: gct[:, hd:hd + 1, :]),
        heads(lambda hd: beta_all[:, :, HEADS + hd:HEADS + hd + 1]),
        state_ref[...], filler)
    state_ref[...] = new_state
    o = o * lax.rsqrt(jnp.mean(o * o, axis=-1, keepdims=True) + EPS) * out_gate
    for hd in range(HEADS):
        mixed_buf[:, :, hd * HEAD_DIM:(hd + 1) * HEAD_DIM] = o[hd * nb:(hd + 1) * nb]
    filler(D_IN_PAD // A_TILE + 1)

    h_buf[...] = _rms_scale(xn_ref[...].reshape(nb * c, D_MODEL)).astype(BF16)

    mixed = mixed_buf[...].reshape(nb * c, D_MODEL).astype(BF16)
    out = xc_ref[...].reshape(nb * c, D_MODEL) + jnp.dot(mixed, wout_ref[...], preferred_element_type=F32)
    o_ref[...] = out.reshape(nb, c, D_MODEL)


def _ffn_kernel(x_ref, p_ref, n2_ref, wg_ref, wu_ref, wd_ref, pg_ref, pp_ref, fg_ref, o_ref, ff_buf,
                *, final_norm):
    x = x_ref[...]
    h = _rms_norm(x, n2_ref[...]).astype(BF16)
    for ci in range(D_FF // FF_CHUNK):
        sl = slice(ci * FF_CHUNK, (ci + 1) * FF_CHUNK)
        gate = jnp.dot(h, wg_ref[:, sl], preferred_element_type=F32)
        up = jnp.dot(h, wu_ref[:, sl], preferred_element_type=F32)
        ff_buf[:, sl] = (_silu(gate) * up).astype(BF16)
    x = x + jnp.dot(ff_buf[...], wd_ref[...], preferred_element_type=F32)
    gate = _sigmoid(jnp.dot(x.astype(BF16), pg_ref[...], preferred_element_type=F32))
    emb = jnp.dot(p_ref[...].astype(BF16), pp_ref[...], preferred_element_type=F32)
    x = x + gate * emb
    if final_norm:
        x = _rms_norm(x, fg_ref[...])
    o_ref[...] = x


def _prep_win_kernel(win_ref, g_ref, win_o):
    g = g_ref[...]
    win_o[0:QKV_DIM, :] = (win_ref[0:QKV_DIM, :] * g).astype(BF16)
    win_o[QKV_DIM:AB_LO, :] = (win_ref[QKV_DIM:AB_LO, :] * (0.5 * g)).astype(BF16)
    win_o[AB_LO:D_IN_PAD - AB_PAD, :] = (win_ref[AB_LO + 2 * HEADS:D_IN, :] * g).astype(BF16)
    win_o[D_IN_PAD - AB_PAD:D_IN_PAD, :] = jnp.concatenate(
        [win_ref[AB_LO:AB_LO + 2 * HEADS, :] * g, jnp.zeros((AB_PAD - 2 * HEADS, D_MODEL), F32)],
        axis=0).astype(BF16)


def _prep_kernel(wout_ref, pw_ref, ps_ref, wout_o, pw_o):
    wout_o[...] = wout_ref[...].astype(BF16)
    r = lax.broadcasted_iota(jnp.int32, (POOL_GROUP_DIM, POOL_DIM), 0)
    l = lax.broadcasted_iota(jnp.int32, (POOL_GROUP_DIM, POOL_DIM), 1)
    spread = jnp.where(l % POOL_GROUP_DIM == r, 1.0, 0.0).astype(BF16)
    tiled = jnp.dot(pw_ref[...].astype(BF16), spread, preferred_element_type=F32)
    rr = lax.broadcasted_iota(jnp.int32, (POOL_DIM, POOL_DIM), 0)
    ll = lax.broadcasted_iota(jnp.int32, (POOL_DIM, POOL_DIM), 1)
    pw_o[...] = (jnp.where(rr // POOL_GROUP_DIM == ll // POOL_GROUP_DIM, tiled, 0.0) * ps_ref[...]).astype(BF16)


def _prep_call(w_in, norm1_g, w_out, pool_w, pool_scale):
    depth = w_in.shape[0]
    params = pltpu.CompilerParams(dimension_semantics=("arbitrary", "arbitrary"),
                                  vmem_limit_bytes=VMEM_LIMIT_BYTES)
    w_in_t = jnp.swapaxes(w_in, 1, 2)
    win = pl.pallas_call(
        _prep_win_kernel,
        grid=(depth, 1),
        in_specs=[pl.BlockSpec((None, D_IN, D_MODEL), lambda l, r: (l, 0, 0)),
                  pl.BlockSpec((None, 1, D_MODEL), lambda l, r: (l, 0, 0))],
        out_specs=pl.BlockSpec((None, D_IN_PAD, D_MODEL), lambda l, r: (l, 0, 0)),
        out_shape=jax.ShapeDtypeStruct((depth, D_IN_PAD, D_MODEL), BF16),
        compiler_params=params,
        name="prep_win",
    )(w_in_t, norm1_g)

    pool_w = pool_w.reshape(depth, POOL_DIM, POOL_GROUP_DIM)
    wout, pw = pl.pallas_call(
        _prep_kernel,
        grid=(depth, 1),
        in_specs=[pl.BlockSpec((None, D_MODEL, D_MODEL), lambda l, r: (l, 0, 0)),
                  pl.BlockSpec((None, POOL_DIM, POOL_GROUP_DIM), lambda l, r: (l, 0, 0)),
                  pl.BlockSpec((None, 1, POOL_DIM), lambda l, r: (l, 0, 0))],
        out_specs=[pl.BlockSpec((None, D_MODEL, D_MODEL), lambda l, r: (l, 0, 0)),
                   pl.BlockSpec((None, POOL_DIM, POOL_DIM), lambda l, r: (l, 0, 0))],
        out_shape=[jax.ShapeDtypeStruct((depth, D_MODEL, D_MODEL), BF16),
                   jax.ShapeDtypeStruct((depth, POOL_DIM, POOL_DIM), BF16)],
        compiler_params=params,
        name="prep",
    )(w_out, pool_w, pool_scale)
    return win, wout, pw


def _layer_spec(a, layer):
    nd = a.ndim - 1
    return pl.BlockSpec((None,) + a.shape[1:], lambda *_: (layer,) + (0,) * nd,
                        pipeline_mode=pl.Buffered(1))


_SMEM_SPEC = pl.BlockSpec(memory_space=pltpu.SMEM)


def _cast_spec(a, layer, n_blocks, axis):
    shape = [None, a.shape[1], a.shape[2]]
    shape[axis] = a.shape[axis] // n_blocks

    def index(s):
        idx = [layer, 0, 0]
        idx[axis] = jnp.minimum(s, n_blocks - 1)
        return tuple(idx)

    return pl.BlockSpec(tuple(shape), index)


def _mixer_call(x, layer, win, cw, alog, dtb, on, pw, sw, wout, ffn_weights):
    nb, seq, _ = x.shape
    c = CHUNK
    n_chunks = seq // c
    nxt = pl.BlockSpec((nb, c, D_MODEL), lambda s: (0, jnp.minimum(s + 1, n_chunks - 1), 0))
    cur = pl.BlockSpec((nb, c, D_MODEL), lambda s: (0, jnp.maximum(s - 1, 0), 0))
    vmem_consts = (win, cw, on, pw, sw, wout)
    w_gate, w_up, w_down, ple_gate, ple_proj = ffn_weights
    cast_plan = ((w_gate, 32, 1), (w_up, 32, 1), (w_down, 8, 2), (ple_gate, 32, 1), (ple_proj, 16, 1))
    cast_specs = [_cast_spec(a, layer, n, ax) for a, n, ax in cast_plan]
    cast_specs_out = [_cast_spec(a, 0, n, ax) for a, n, ax in cast_plan]
    outs = pl.pallas_call(
        functools.partial(_mixer_kernel, layer=layer),
        grid=(n_chunks + 1,),
        in_specs=[nxt, cur, _SMEM_SPEC, _SMEM_SPEC] + [_layer_spec(a, layer) for a in vmem_consts]
                 + cast_specs,
        out_specs=[cur] + cast_specs_out,
        out_shape=[jax.ShapeDtypeStruct(x.shape, F32)]
                  + [jax.ShapeDtypeStruct((1,) + a.shape[1:], BF16) for a, _, _ in cast_plan],
        scratch_shapes=[
            pltpu.VMEM((nb * c, D_MODEL), BF16),
            pltpu.VMEM((nb, QKV_HIST + c, QKV_DIM), F32),
            pltpu.VMEM((nb, c, REST_DIM), F32),
            pltpu.VMEM((nb, POOL_HIST, POOL_DIM), F32),
            pltpu.VMEM((nb, QKV_HIST, CONV_DIM), F32),
            pltpu.VMEM((nb, c, D_MODEL), F32),
            pltpu.VMEM((HEADS * nb, HEAD_DIM, HEAD_DIM), F32),
        ],
        compiler_params=pltpu.CompilerParams(
            dimension_semantics=("arbitrary",), vmem_limit_bytes=VMEM_LIMIT_BYTES),
        name="mixer",
    )(x, x, alog, dtb, *vmem_consts, w_gate, w_up, w_down, ple_gate, ple_proj)
    return outs[0], outs[1:]


def _ffn_call(x2d, p3d, layer, n2, ffn_weights, fg, final_norm):
    tokens = x2d.shape[0]
    return pl.pallas_call(
        functools.partial(_ffn_kernel, final_norm=final_norm),
        grid=(tokens // FFN_TILE,),
        in_specs=[pl.BlockSpec((FFN_TILE, D_MODEL), lambda i: (i, 0)),
                  pl.BlockSpec((None, FFN_TILE, PLE_DIM), lambda i: (layer, i, 0))]
                 + [_layer_spec(n2, layer)] + [_layer_spec(a, 0) for a in ffn_weights]
                 + [pl.BlockSpec(fg.shape, lambda i: (0, 0), pipeline_mode=pl.Buffered(1))],
        out_specs=pl.BlockSpec((FFN_TILE, D_MODEL), lambda i: (i, 0)),
        out_shape=jax.ShapeDtypeStruct(x2d.shape, F32),
        scratch_shapes=[pltpu.VMEM((FFN_TILE, D_FF), BF16)],
        compiler_params=pltpu.CompilerParams(
            dimension_semantics=("arbitrary",), vmem_limit_bytes=VMEM_LIMIT_BYTES),
        name="ffn",
    )(x2d, p3d, n2, *ffn_weights, fg)


def kernel(x, p, norm1_g, w_in, conv_qkv, a_log, dt_bias, onorm_g, pool_w, pool_scale, sconv_w, w_out,
           norm2_g, w_gate, w_up, w_down, ple_proj, ple_gate, final_g):
    depth = w_in.shape[0]
    nb, seq, _ = x.shape
    n1, on, ps, n2 = (a.reshape(depth, 1, a.shape[-1]) for a in (norm1_g, onorm_g, pool_scale, norm2_g))
    win, wout, pw = _prep_call(w_in, n1, w_out, pool_w, ps)
    p3d = p.reshape(depth, nb * seq, PLE_DIM)
    fg = final_g.reshape(1, D_MODEL)
    for i in range(depth):
        x, ffn_weights = _mixer_call(x, i, win, conv_qkv, a_log, dt_bias, on, pw, sconv_w, wout,
                                     (w_gate, w_up, w_down, ple_gate, ple_proj))
        x = _ffn_call(x.reshape(nb * seq, D_MODEL), p3d, i, n2, ffn_weights, fg,
                      final_norm=(i == depth - 1)).reshape(nb, seq, D_MODEL)
    return x
```

```python
import functools

import jax
import jax.numpy as jnp
from jax import lax
from jax.experimental import pallas as pl
from jax.experimental.pallas import tpu as pltpu

D_MODEL = 1024
PLE_DIM = 256
EPS = 1e-6
HEAD_DIM = 128
HEADS = 4
A_DIM = HEADS * HEAD_DIM
QKV_DIM = 3 * A_DIM
QKV_CONV_WIDTH = 4
CHUNK = 64
POOL_WINDOWS = (2, 4, 8, 16)
POOL_DIM = 256
POOL_GROUP_DIM = 64
CONV_DIM = 256
CONV_WIDTH = 3
D_FF = 2816
AB_PAD = 128
REST_DIM = A_DIM + POOL_DIM + 3 * CONV_DIM + AB_PAD
Z_OFF, POOL_OFF, CB_OFF, CC_OFF, CH_OFF, AB_OFF = 0, 512, 768, 1024, 1280, 1536
D_IN_PAD = QKV_DIM + REST_DIM
AB_LO = 4 * A_DIM
D_IN = AB_LO + 2 * HEADS + POOL_DIM + 3 * CONV_DIM

QKV_HIST = 8
POOL_HIST = 16
VMEM_LIMIT_BYTES = 56 * 1024 * 1024
FFN_TILE = 1024
FF_CHUNK = 256
A_TILE = 256

BF16 = jnp.bfloat16
F32 = jnp.float32


def _sigmoid(x):
    return 0.5 * jnp.tanh(0.5 * x) + 0.5


def _silu(x):
    hx = 0.5 * x
    return hx + hx * jnp.tanh(hx)


def _silu_of_double(hx):
    return hx + hx * jnp.tanh(hx)


def _softplus(x):
    return jnp.maximum(x, 0.0) + jnp.log1p(jnp.exp(-jnp.abs(x)))


def _rms_scale(x):
    return x * lax.rsqrt(jnp.mean(x * x, axis=-1, keepdims=True) + EPS)


def _rms_norm(x, g):
    return _rms_scale(x) * g


def _bmm(a, b):
    return lax.dot_general(a.astype(BF16), b.astype(BF16), (((2,), (1,)), ((0,), (0,))),
                           preferred_element_type=F32)


def _bmm_nt(a, b):
    return lax.dot_general(a.astype(BF16), b.astype(BF16), (((2,), (2,)), ((0,), (0,))),
                           preferred_element_type=F32)


def _delta_chunk(qr, kr, v, gcol, grow, beta, eg, kd_scale, state, filler):
    c = CHUNK
    rq = lax.rsqrt(jnp.sum(qr * qr, axis=-1, keepdims=True) + EPS) * (HEAD_DIM ** -0.5)
    rk = lax.rsqrt(jnp.sum(kr * kr, axis=-1, keepdims=True) + EPS)
    q = qr * rq
    k = kr * rk
    g_last = gcol[:, c - 1:c, :]
    kb = k * beta
    vb = v * beta
    row = lax.broadcasted_iota(jnp.int32, (1, c, c), 1)
    col = lax.broadcasted_iota(jnp.int32, (1, c, c), 2)
    decay = jnp.exp(jnp.where(row >= col, gcol - grow, -1e30))
    kq = _bmm_nt(jnp.concatenate([kb, q], axis=1), k)
    lower = jnp.where(row > col, kq[:, 0:c] * decay, 0.0)
    attn = kq[:, c:2 * c] * decay
    filler(1)
    x = -lower
    t = jnp.where(row == col, 1.0, 0.0) + x
    x = _bmm(x, x)
    n_sq = CHUNK.bit_length() - 2
    for _ in range(n_sq - 1):
        tx = _bmm(jnp.concatenate([x, t], axis=1), x)
        x = tx[:, 0:c]
        t = t + tx[:, c:2 * c]
        filler(1)
    t = t + _bmm(t, x)
    uw = _bmm(t, jnp.concatenate([vb, kb * eg], axis=2))
    ws = _bmm(jnp.concatenate([uw[:, :, HEAD_DIM:2 * HEAD_DIM], q * eg], axis=1), state)
    v_new = uw[:, :, 0:HEAD_DIM] - ws[:, 0:c]
    o = ws[:, c:2 * c] + _bmm(attn, v_new)
    kd = k * kd_scale
    new_state = state * jnp.exp(g_last) + _bmm(jnp.swapaxes(kd, 1, 2), v_new)
    return o, new_state


def _project_tile(h_buf, win_ref, qkv_ext, rest_buf, lo, nb):
    c = CHUNK
    hi = min(lo + A_TILE, D_IN_PAD)
    tile = lax.dot_general(h_buf[...], win_ref[lo:hi, :], (((1,), (1,)), ((), ())),
                           preferred_element_type=F32).reshape(nb, c, hi - lo)
    if hi <= QKV_DIM:
        qkv_ext[:, QKV_HIST:QKV_HIST + c, lo:hi] = tile
    else:
        rest_buf[:, :, lo - QKV_DIM:hi - QKV_DIM] = tile


def _mixer_kernel(xn_ref, xc_ref, alog_ref, dtb_ref, win_ref, cw_ref, on_ref, pw_ref,
                  sw_ref, wout_ref, wg_ref, wu_ref, wd_ref, pg_ref, pp_ref,
                  o_ref, wg_o, wu_o, wd_o, pg_o, pp_o,
                  h_buf, qkv_ext, rest_buf, pool_hist, m_hist, mixed_buf, state_ref, *, layer):
    s = pl.program_id(0)
    nb = xn_ref.shape[0]
    c = CHUNK

    for src_ref, dst_ref in ((wg_ref, wg_o), (wu_ref, wu_o), (wd_ref, wd_o), (pg_ref, pg_o), (pp_ref, pp_o)):
        dst_ref[...] = src_ref[...].astype(BF16)

    @pl.when(s == 0)
    def _():
        qkv_ext[:, 0:QKV_HIST, :] = jnp.zeros((nb, QKV_HIST, QKV_DIM), F32)
        pool_hist[...] = jnp.zeros(pool_hist.shape, F32)
        m_hist[...] = jnp.zeros(m_hist.shape, F32)
        state_ref[...] = jnp.zeros(state_ref.shape, F32)
        o_ref[...] = xc_ref[...]
        h_buf[...] = _rms_scale(xc_ref[...].reshape(nb * c, D_MODEL)).astype(BF16)
        for lo in range(0, D_IN_PAD, A_TILE):
            _project_tile(h_buf, win_ref, qkv_ext, rest_buf, lo, nb)
        h_buf[...] = _rms_scale(xn_ref[...].reshape(nb * c, D_MODEL)).astype(BF16)

    @pl.when(s > 0)
    def _():
        _mixer_step(s, xn_ref, xc_ref, alog_ref, dtb_ref, win_ref, cw_ref, on_ref, pw_ref,
                    sw_ref, wout_ref, o_ref,
                    h_buf, qkv_ext, rest_buf, pool_hist, m_hist, mixed_buf, state_ref, layer)


def _mixer_step(s, xn_ref, xc_ref, alog_ref, dtb_ref, win_ref, cw_ref, on_ref, pw_ref,
                sw_ref, wout_ref, o_ref,
                h_buf, qkv_ext, rest_buf, pool_hist, m_hist, mixed_buf, state_ref, layer):
    nb = xn_ref.shape[0]
    c = CHUNK

    rest_tiles = [QKV_DIM + off for off in (POOL_OFF, CB_OFF, CC_OFF, CH_OFF, AB_OFF, Z_OFF, Z_OFF + A_TILE)]
    a_tiles = iter(list(range(0, QKV_DIM, A_TILE)) + rest_tiles)

    def filler(n):
        for _ in range(n):
            lo = next(a_tiles, None)
            if lo is None:
                return
            _project_tile(h_buf, win_ref, qkv_ext, rest_buf, lo, nb)


    act_tiles = []
    for lo_c in range(0, QKV_DIM, A_TILE):
        cols = slice(lo_c, lo_c + A_TILE)
        ext = qkv_ext[:, :, cols]
        taps = 0.5 * cw_ref[:, cols]
        acc = ext * taps[QKV_CONV_WIDTH - 1:QKV_CONV_WIDTH]
        for j in range(QKV_CONV_WIDTH - 1):
            acc = acc + pltpu.roll(ext, QKV_CONV_WIDTH - 1 - j, 1) * taps[j:j + 1]
        qkv_ext[:, 0:QKV_HIST, cols] = ext[:, c:c + QKV_HIST, :]
        filler(1)
        act_tiles.append(_silu_of_double(acc[:, QKV_HIST:QKV_HIST + c, :]))
    act = jnp.concatenate(act_tiles, axis=2)

    hp = rest_buf[:, :, POOL_OFF:POOL_OFF + POOL_DIM]
    pext = jnp.concatenate([pool_hist[...], hp], axis=1)
    group = lax.broadcasted_iota(jnp.int32, (1, c, POOL_DIM), 2) // POOL_GROUP_DIM
    sl = slice(POOL_HIST, POOL_HIST + c)
    running, width = pext, 1
    sums = win = None
    for gi, window in enumerate(POOL_WINDOWS):
        while width < window:
            running = running + pltpu.roll(running, width, 1)
            width *= 2
        sums = running[:, sl] if sums is None else jnp.where(group >= gi, running[:, sl], sums)
        win = float(window) if win is None else jnp.where(group >= gi, float(window), win)
    pos = ((s - 1) * c + 1 + lax.broadcasted_iota(jnp.int32, (1, c, POOL_DIM), 1)).astype(F32)
    pooled = sums * (1.0 / jnp.minimum(pos, win)) - hp
    y = jnp.dot(pooled.reshape(nb * c, POOL_DIM).astype(BF16), pw_ref[...], preferred_element_type=F32)
    mixed_buf[:, :, A_DIM:A_DIM + POOL_DIM] = y.reshape(nb, c, POOL_DIM)
    pool_hist[...] = pext[:, c:c + POOL_HIST, :]
    filler(1)

    mext = jnp.concatenate([m_hist[...], rest_buf[:, :, CC_OFF:CC_OFF + CONV_DIM]
                            * rest_buf[:, :, CH_OFF:CH_OFF + CONV_DIM]], axis=1)
    yc = mext * sw_ref[CONV_WIDTH - 1:CONV_WIDTH, :]
    for j in range(CONV_WIDTH - 2, -1, -1):
        yc = yc + pltpu.roll(mext, CONV_WIDTH - 1 - j, 1) * sw_ref[j:j + 1, :]
    mixed_buf[:, :, A_DIM + POOL_DIM:D_MODEL] = (rest_buf[:, :, CB_OFF:CB_OFF + CONV_DIM]
                                                 * yc[:, QKV_HIST:QKV_HIST + c, :])
    m_hist[...] = mext[:, c:c + QKV_HIST, :]
    filler(3)

    ab = rest_buf[:, :, AB_OFF:AB_OFF + AB_PAD]
    filler(1)
    pairs = nb // 2
    abt = jnp.swapaxes(ab.reshape(pairs, 2 * c, AB_PAD), 1, 2)[:, 0:2 * HEADS, :]
    row_t = lax.broadcasted_iota(jnp.int32, (1, 2 * HEADS, 2 * c), 1)
    lane_t = lax.broadcasted_iota(jnp.int32, (1, 2 * HEADS, 2 * c), 2)
    tpos = lane_t % c
    alog = jnp.zeros((1, 2 * HEADS, 2 * c), F32)
    dtb = jnp.zeros((1, 2 * HEADS, 2 * c), F32)
    for hd in range(HEADS):
        alog = jnp.where(row_t == hd, alog_ref[layer, hd], alog)
        dtb = jnp.where(row_t == hd, dtb_ref[layer, hd], dtb)
    gc_t = -jnp.exp(alog) * _softplus(abt + dtb)
    beta_t = _sigmoid(abt)
    sh = 1
    while sh < c:
        gc_t = gc_t + jnp.where(tpos >= sh, pltpu.roll(gc_t, sh, 2), 0.0)
        sh *= 2
    g_last_t = jnp.where(lane_t < c, gc_t[:, :, c - 1:c], gc_t[:, :, 2 * c - 1:2 * c])
    cols_t = jnp.concatenate(
        [gc_t[:, 0:HEADS], beta_t[:, HEADS:2 * HEADS], jnp.exp(gc_t)[:, 0:HEADS],
         jnp.exp(g_last_t - gc_t)[:, 0:HEADS], jnp.zeros((pairs, AB_PAD - 4 * HEADS, 2 * c), F32)], axis=1)
    cols = jnp.swapaxes(cols_t, 1, 2).reshape(nb, c, AB_PAD)
    gct = jnp.stack([gc_t[:, :, 0:c], gc_t[:, :, c:2 * c]], axis=1).reshape(nb, 2 * HEADS, c)

    def heads(f):
        return jnp.concatenate([f(hd) for hd in range(HEADS)], axis=0)

    out_gate = on_ref[...] * _silu_of_double(
        heads(lambda hd: rest_buf[:, :, Z_OFF + hd * HEAD_DIM:Z_OFF + (hd + 1) * HEAD_DIM]))
    filler(2)
    o, new_state = _delta_chunk(
        heads(lambda hd: act[:, :, hd * HEAD_DIM:(hd + 1) * HEAD_DIM]),
        heads(lambda hd: act[:, :, A_DIM + hd * HEAD_DIM:A_DIM + (hd + 1) * HEAD_DIM]),
        heads(lambda hd: act[:, :, 2 * A_DIM + hd * HEAD_DIM:2 * A_DIM + (hd + 1) * HEAD_DIM]),
        heads(lambda hd: cols[:, :, hd:hd + 1]),
        heads(lambda hd: gct[:, hd:hd + 1, :]),
        heads(lambda hd: cols[:, :, HEADS + hd:HEADS + hd + 1]),
        heads(lambda hd: cols[:, :, 2 * HEADS + hd:2 * HEADS + hd + 1]),
        heads(lambda hd: cols[:, :, 3 * HEADS + hd:3 * HEADS + hd + 1]),
        state_ref[...], filler)
    state_ref[...] = new_state
    o = o * lax.rsqrt(jnp.mean(o * o, axis=-1, keepdims=True) + EPS) * out_gate
    for hd in range(HEADS):
        mixed_buf[:, :, hd * HEAD_DIM:(hd + 1) * HEAD_DIM] = o[hd * nb:(hd + 1) * nb]
    filler(D_IN_PAD // A_TILE + 1)

    h_buf[...] = _rms_scale(xn_ref[...].reshape(nb * c, D_MODEL)).astype(BF16)

    mixed = mixed_buf[...].reshape(nb * c, D_MODEL).astype(BF16)
    out = xc_ref[...].reshape(nb * c, D_MODEL) + jnp.dot(mixed, wout_ref[...], preferred_element_type=F32)
    o_ref[...] = out.reshape(nb, c, D_MODEL)


def _ffn_kernel(x_ref, p_ref, n2_ref, wg_ref, wu_ref, wd_ref, pg_ref, pp_ref, fg_ref, o_ref, ff_buf,
                *, final_norm):
    x = x_ref[...]
    h = _rms_norm(x, n2_ref[...]).astype(BF16)
    for ci in range(D_FF // FF_CHUNK):
        sl = slice(ci * FF_CHUNK, (ci + 1) * FF_CHUNK)
        gate = jnp.dot(h, wg_ref[:, sl], preferred_element_type=F32)
        up = jnp.dot(h, wu_ref[:, sl], preferred_element_type=F32)
        ff_buf[:, sl] = (_silu(gate) * up).astype(BF16)
    x = x + jnp.dot(ff_buf[...], wd_ref[...], preferred_element_type=F32)
    gate = _sigmoid(jnp.dot(x.astype(BF16), pg_ref[...], preferred_element_type=F32))
    emb = jnp.dot(p_ref[...].astype(BF16), pp_ref[...], preferred_element_type=F32)
    x = x + gate * emb
    if final_norm:
        x = _rms_norm(x, fg_ref[...])
    o_ref[...] = x


def _prep_win_kernel(win_ref, g_ref, win_o):
    g = g_ref[...]
    win_o[0:QKV_DIM, :] = (win_ref[0:QKV_DIM, :] * g).astype(BF16)
    win_o[QKV_DIM:AB_LO, :] = (win_ref[QKV_DIM:AB_LO, :] * (0.5 * g)).astype(BF16)
    win_o[AB_LO:D_IN_PAD - AB_PAD, :] = (win_ref[AB_LO + 2 * HEADS:D_IN, :] * g).astype(BF16)
    win_o[D_IN_PAD - AB_PAD:D_IN_PAD, :] = jnp.concatenate(
        [win_ref[AB_LO:AB_LO + 2 * HEADS, :] * g, jnp.zeros((AB_PAD - 2 * HEADS, D_MODEL), F32)],
        axis=0).astype(BF16)


def _prep_kernel(wout_ref, pw_ref, ps_ref, wout_o, pw_o):
    wout_o[...] = wout_ref[...].astype(BF16)
    r = lax.broadcasted_iota(jnp.int32, (POOL_GROUP_DIM, POOL_DIM), 0)
    l = lax.broadcasted_iota(jnp.int32, (POOL_GROUP_DIM, POOL_DIM), 1)
    spread = jnp.where(l % POOL_GROUP_DIM == r, 1.0, 0.0).astype(BF16)
    tiled = jnp.dot(pw_ref[...].astype(BF16), spread, preferred_element_type=F32)
    rr = lax.broadcasted_iota(jnp.int32, (POOL_DIM, POOL_DIM), 0)
    ll = lax.broadcasted_iota(jnp.int32, (POOL_DIM, POOL_DIM), 1)
    pw_o[...] = (jnp.where(rr // POOL_GROUP_DIM == ll // POOL_GROUP_DIM, tiled, 0.0) * ps_ref[...]).astype(BF16)


def _prep_call(w_in, norm1_g, w_out, pool_w, pool_scale):
    depth = w_in.shape[0]
    params = pltpu.CompilerParams(dimension_semantics=("arbitrary", "arbitrary"),
                                  vmem_limit_bytes=VMEM_LIMIT_BYTES)
    w_in_t = jnp.swapaxes(w_in, 1, 2)
    win = pl.pallas_call(
        _prep_win_kernel,
        grid=(depth, 1),
        in_specs=[pl.BlockSpec((None, D_IN, D_MODEL), lambda l, r: (l, 0, 0)),
                  pl.BlockSpec((None, 1, D_MODEL), lambda l, r: (l, 0, 0))],
        out_specs=pl.BlockSpec((None, D_IN_PAD, D_MODEL), lambda l, r: (l, 0, 0)),
        out_shape=jax.ShapeDtypeStruct((depth, D_IN_PAD, D_MODEL), BF16),
        compiler_params=params,
        name="prep_win",
    )(w_in_t, norm1_g)

    pool_w = pool_w.reshape(depth, POOL_DIM, POOL_GROUP_DIM)
    wout, pw = pl.pallas_call(
        _prep_kernel,
        grid=(depth, 1),
        in_specs=[pl.BlockSpec((None, D_MODEL, D_MODEL), lambda l, r: (l, 0, 0)),
                  pl.BlockSpec((None, POOL_DIM, POOL_GROUP_DIM), lambda l, r: (l, 0, 0)),
                  pl.BlockSpec((None, 1, POOL_DIM), lambda l, r: (l, 0, 0))],
        out_specs=[pl.BlockSpec((None, D_MODEL, D_MODEL), lambda l, r: (l, 0, 0)),
                   pl.BlockSpec((None, POOL_DIM, POOL_DIM), lambda l, r: (l, 0, 0))],
        out_shape=[jax.ShapeDtypeStruct((depth, D_MODEL, D_MODEL), BF16),
                   jax.ShapeDtypeStruct((depth, POOL_DIM, POOL_DIM), BF16)],
        compiler_params=params,
        name="prep",
    )(w_out, pool_w, pool_scale)
    return win, wout, pw


def _layer_spec(a, layer):
    nd = a.ndim - 1
    return pl.BlockSpec((None,) + a.shape[1:], lambda *_: (layer,) + (0,) * nd,
                        pipeline_mode=pl.Buffered(1))


_SMEM_SPEC = pl.BlockSpec(memory_space=pltpu.SMEM)


def _cast_spec(a, layer, n_blocks, axis):
    shape = [None, a.shape[1], a.shape[2]]
    shape[axis] = a.shape[axis] // n_blocks

    def index(s):
        idx = [layer, 0, 0]
        idx[axis] = jnp.minimum(s, n_blocks - 1)
        return tuple(idx)

    return pl.BlockSpec(tuple(shape), index)


def _mixer_call(x, layer, win, cw, alog, dtb, on, pw, sw, wout, ffn_weights):
    nb, seq, _ = x.shape
    c = CHUNK
    n_chunks = seq // c
    nxt = pl.BlockSpec((nb, c, D_MODEL), lambda s: (0, jnp.minimum(s + 1, n_chunks - 1), 0))
    cur = pl.BlockSpec((nb, c, D_MODEL), lambda s: (0, jnp.maximum(s - 1, 0), 0))
    vmem_consts = (win, cw, on, pw, sw, wout)
    w_gate, w_up, w_down, ple_gate, ple_proj = ffn_weights
    cast_plan = ((w_gate, 32, 1), (w_up, 32, 1), (w_down, 8, 2), (ple_gate, 32, 1), (ple_proj, 16, 1))
    cast_specs = [_cast_spec(a, layer, n, ax) for a, n, ax in cast_plan]
    cast_specs_out = [_cast_spec(a, 0, n, ax) for a, n, ax in cast_plan]
    outs = pl.pallas_call(
        functools.partial(_mixer_kernel, layer=layer),
        grid=(n_chunks + 1,),
        in_specs=[nxt, cur, _SMEM_SPEC, _SMEM_SPEC] + [_layer_spec(a, layer) for a in vmem_consts]
                 + cast_specs,
        out_specs=[cur] + cast_specs_out,
        out_shape=[jax.ShapeDtypeStruct(x.shape, F32)]
                  + [jax.ShapeDtypeStruct((1,) + a.shape[1:], BF16) for a, _, _ in cast_plan],
        scratch_shapes=[
            pltpu.VMEM((nb * c, D_MODEL), BF16),
            pltpu.VMEM((nb, QKV_HIST + c, QKV_DIM), F32),
            pltpu.VMEM((nb, c, REST_DIM), F32),
            pltpu.VMEM((nb, POOL_HIST, POOL_DIM), F32),
            pltpu.VMEM((nb, QKV_HIST, CONV_DIM), F32),
            pltpu.VMEM((nb, c, D_MODEL), F32),
            pltpu.VMEM((HEADS * nb, HEAD_DIM, HEAD_DIM), F32),
        ],
        compiler_params=pltpu.CompilerParams(
            dimension_semantics=("arbitrary",), vmem_limit_bytes=VMEM_LIMIT_BYTES),
        name="mixer",
    )(x, x, alog, dtb, *vmem_consts, w_gate, w_up, w_down, ple_gate, ple_proj)
    return outs[0], outs[1:]


def _ffn_call(x2d, p3d, layer, n2, ffn_weights, fg, final_norm):
    tokens = x2d.shape[0]
    return pl.pallas_call(
        functools.partial(_ffn_kernel, final_norm=final_norm),
        grid=(tokens // FFN_TILE,),
        in_specs=[pl.BlockSpec((FFN_TILE, D_MODEL), lambda i: (i, 0)),
                  pl.BlockSpec((None, FFN_TILE, PLE_DIM), lambda i: (layer, i, 0))]
                 + [_layer_spec(n2, layer)] + [_layer_spec(a, 0) for a in ffn_weights]
                 + [pl.BlockSpec(fg.shape, lambda i: (0, 0), pipeline_mode=pl.Buffered(1))],
        out_specs=pl.BlockSpec((FFN_TILE, D_MODEL), lambda i: (i, 0)),
        out_shape=jax.ShapeDtypeStruct(x2d.shape, F32),
        scratch_shapes=[pltpu.VMEM((FFN_TILE, D_FF), BF16)],
        compiler_params=pltpu.CompilerParams(
            dimension_semantics=("arbitrary",), vmem_limit_bytes=VMEM_LIMIT_BYTES),
        name="ffn",
    )(x2d, p3d, n2, *ffn_weights, fg)


def kernel(x, p, norm1_g, w_in, conv_qkv, a_log, dt_bias, onorm_g, pool_w, pool_scale, sconv_w, w_out,
           norm2_g, w_gate, w_up, w_down, ple_proj, ple_gate, final_g):
    depth = w_in.shape[0]
    nb, seq, _ = x.shape
    n1, on, ps, n2 = (a.reshape(depth, 1, a.shape[-1]) for a in (norm1_g, onorm_g, pool_scale, norm2_g))
    win, wout, pw = _prep_call(w_in, n1, w_out, pool_w, ps)
    p3d = p.reshape(depth, nb * seq, PLE_DIM)
    fg = final_g.reshape(1, D_MODEL)
    for i in range(depth):
        x, ffn_weights = _mixer_call(x, i, win, conv_qkv, a_log, dt_bias, on, pw, sconv_w, wout,
                                     (w_gate, w_up, w_down, ple_gate, ple_proj))
        x = _ffn_call(x.reshape(nb * seq, D_MODEL), p3d, i, n2, ffn_weights, fg,
                      final_norm=(i == depth - 1)).reshape(nb, seq, D_MODEL)
    return x
```

```python
import functools

import jax
import jax.numpy as jnp
from jax import lax
from jax.experimental import pallas as pl
from jax.experimental.pallas import tpu as pltpu

D_MODEL = 1024
PLE_DIM = 256
EPS = 1e-6
HEAD_DIM = 128
HEADS = 4
A_DIM = HEADS * HEAD_DIM
QKV_DIM = 3 * A_DIM
QKV_CONV_WIDTH = 4
CHUNK = 64
POOL_WINDOWS = (2, 4, 8, 16)
POOL_DIM = 256
POOL_GROUP_DIM = 64
CONV_DIM = 256
CONV_WIDTH = 3
D_FF = 2816
AB_PAD = 128
REST_DIM = A_DIM + POOL_DIM + 3 * CONV_DIM + AB_PAD
Z_OFF, POOL_OFF, CB_OFF, CC_OFF, CH_OFF, AB_OFF = 0, 512, 768, 1024, 1280, 1536
D_IN_PAD = QKV_DIM + REST_DIM
AB_LO = 4 * A_DIM
D_IN = AB_LO + 2 * HEADS + POOL_DIM + 3 * CONV_DIM

QKV_HIST = 8
POOL_HIST = 16
VMEM_LIMIT_BYTES = 56 * 1024 * 1024
FFN_TILE = 1024
FF_CHUNK = 256
A_TILE = 256

BF16 = jnp.bfloat16
F32 = jnp.float32


def _sigmoid(x):
    return 0.5 * jnp.tanh(0.5 * x) + 0.5


def _silu(x):
    hx = 0.5 * x
    return hx + hx * jnp.tanh(hx)


def _silu_of_double(hx):
    return hx + hx * jnp.tanh(hx)


def _softplus(x):
    return jnp.maximum(x, 0.0) + jnp.log1p(jnp.exp(-jnp.abs(x)))


def _rms_scale(x):
    return x * lax.rsqrt(jnp.mean(x * x, axis=-1, keepdims=True) + EPS)


def _rms_norm(x, g):
    return _rms_scale(x) * g


def _bmm(a, b):
    return lax.dot_general(a.astype(BF16), b.astype(BF16), (((2,), (1,)), ((0,), (0,))),
                           preferred_element_type=F32)


def _bmm_nt(a, b):
    return lax.dot_general(a.astype(BF16), b.astype(BF16), (((2,), (2,)), ((0,), (0,))),
                           preferred_element_type=F32)


def _delta_chunk(qr, kr, v, gcol, grow, beta, eg, kd_scale, state, filler):
    c = CHUNK
    rq = lax.rsqrt(jnp.sum(qr * qr, axis=-1, keepdims=True) + EPS) * (HEAD_DIM ** -0.5)
    rk = lax.rsqrt(jnp.sum(kr * kr, axis=-1, keepdims=True) + EPS)
    q = qr * rq
    k = kr * rk
    g_last = gcol[:, c - 1:c, :]
    kb = k * beta
    vb = v * beta
    row = lax.broadcasted_iota(jnp.int32, (1, c, c), 1)
    col = lax.broadcasted_iota(jnp.int32, (1, c, c), 2)
    decay = jnp.exp(jnp.where(row >= col, gcol - grow, -1e30))
    kq = _bmm_nt(jnp.concatenate([kb, q], axis=1), k)
    lower = jnp.where(row > col, kq[:, 0:c] * decay, 0.0)
    attn = kq[:, c:2 * c] * decay
    filler(1)
    x = -lower
    t = jnp.where(row == col, 1.0, 0.0) + x
    x = _bmm(x, x)
    n_sq = CHUNK.bit_length() - 2
    for _ in range(n_sq - 1):
        tx = _bmm(jnp.concatenate([x, t], axis=1), x)
        x = tx[:, 0:c]
        t = t + tx[:, c:2 * c]
        filler(1)
    t = t + _bmm(t, x)
    uw = _bmm(t, jnp.concatenate([vb, kb * eg], axis=2))
    ws = _bmm(jnp.concatenate([uw[:, :, HEAD_DIM:2 * HEAD_DIM], q * eg], axis=1), state)
    v_new = uw[:, :, 0:HEAD_DIM] - ws[:, 0:c]
    o = ws[:, c:2 * c] + _bmm(attn, v_new)
    kd = k * kd_scale
    new_state = state * jnp.exp(g_last) + _bmm(jnp.swapaxes(kd, 1, 2), v_new)
    return o, new_state


def _project_tile(h_buf, win_ref, cw_ref, qkv_hist, act_buf, rest_buf, lo, nb):
    c = CHUNK
    hi = min(lo + A_TILE, D_IN_PAD)
    tile = lax.dot_general(h_buf[...], win_ref[lo:hi, :], (((1,), (1,)), ((), ())),
                           preferred_element_type=F32).reshape(nb, c, hi - lo)
    if hi > QKV_DIM:
        rest_buf[:, :, lo - QKV_DIM:hi - QKV_DIM] = tile
        return
    cols = slice(lo, hi)
    ext = jnp.concatenate([qkv_hist[:, :, cols], tile], axis=1)
    taps = 0.5 * cw_ref[:, cols]
    acc = ext * taps[QKV_CONV_WIDTH - 1:QKV_CONV_WIDTH]
    for j in range(QKV_CONV_WIDTH - 1):
        acc = acc + pltpu.roll(ext, QKV_CONV_WIDTH - 1 - j, 1) * taps[j:j + 1]
    qkv_hist[:, :, cols] = tile[:, c - QKV_HIST:c, :]
    act_buf[:, :, cols] = _silu_of_double(acc[:, QKV_HIST:QKV_HIST + c, :])


def _mixer_kernel(xn_ref, xc_ref, alog_ref, dtb_ref, win_ref, cw_ref, on_ref, pw_ref,
                  sw_ref, wout_ref, wg_ref, wu_ref, wd_ref, pg_ref, pp_ref,
                  o_ref, wg_o, wu_o, wd_o, pg_o, pp_o,
                  h_buf, qkv_hist, act_buf, rest_buf, pool_hist, m_hist, mixed_buf, state_ref, *, layer):
    s = pl.program_id(0)
    nb = xn_ref.shape[0]
    c = CHUNK

    for src_ref, dst_ref in ((wg_ref, wg_o), (wu_ref, wu_o), (wd_ref, wd_o), (pg_ref, pg_o), (pp_ref, pp_o)):
        dst_ref[...] = src_ref[...].astype(BF16)

    @pl.when(s == 0)
    def _():
        qkv_hist[...] = jnp.zeros(qkv_hist.shape, F32)
        pool_hist[...] = jnp.zeros(pool_hist.shape, F32)
        m_hist[...] = jnp.zeros(m_hist.shape, F32)
        state_ref[...] = jnp.zeros(state_ref.shape, F32)
        o_ref[...] = xc_ref[...]
        h_buf[...] = _rms_scale(xc_ref[...].reshape(nb * c, D_MODEL)).astype(BF16)
        for lo in range(0, D_IN_PAD, A_TILE):
            _project_tile(h_buf, win_ref, cw_ref, qkv_hist, act_buf, rest_buf, lo, nb)
        h_buf[...] = _rms_scale(xn_ref[...].reshape(nb * c, D_MODEL)).astype(BF16)

    @pl.when(s > 0)
    def _():
        _mixer_step(s, xn_ref, xc_ref, alog_ref, dtb_ref, win_ref, cw_ref, on_ref, pw_ref,
                    sw_ref, wout_ref, o_ref,
                    h_buf, qkv_hist, act_buf, rest_buf, pool_hist, m_hist, mixed_buf, state_ref, layer)


def _mixer_step(s, xn_ref, xc_ref, alog_ref, dtb_ref, win_ref, cw_ref, on_ref, pw_ref,
                sw_ref, wout_ref, o_ref,
                h_buf, qkv_hist, act_buf, rest_buf, pool_hist, m_hist, mixed_buf, state_ref, layer):
    nb = xn_ref.shape[0]
    c = CHUNK

    rest_tiles = [QKV_DIM + off for off in (POOL_OFF, CB_OFF, CC_OFF, CH_OFF, AB_OFF, Z_OFF, Z_OFF + A_TILE)]
    a_tiles = iter(list(range(0, QKV_DIM, A_TILE)) + rest_tiles)

    def filler(n):
        for _ in range(n):
            lo = next(a_tiles, None)
            if lo is None:
                return
            _project_tile(h_buf, win_ref, cw_ref, qkv_hist, act_buf, rest_buf, lo, nb)

    def heads(f):
        return jnp.concatenate([f(hd) for hd in range(HEADS)], axis=0)


    qr, kr, v = (heads(lambda hd: act_buf[:, :, g * A_DIM + hd * HEAD_DIM:g * A_DIM + (hd + 1) * HEAD_DIM])
                 for g in range(3))
    filler(QKV_DIM // A_TILE)

    hp = rest_buf[:, :, POOL_OFF:POOL_OFF + POOL_DIM]
    pext = jnp.concatenate([pool_hist[...], hp], axis=1)
    s2 = pext + pltpu.roll(pext, 1, 1)
    s4 = s2 + pltpu.roll(s2, 2, 1)
    s8 = s4 + pltpu.roll(s4, 4, 1)
    s16 = s8 + pltpu.roll(s8, 8, 1)
    sl = slice(POOL_HIST, POOL_HIST + c)
    lane_p = lax.broadcasted_iota(jnp.int32, (1, c, POOL_DIM), 2)
    sums = jnp.where(lane_p < 64, s2[:, sl], jnp.where(lane_p < 128, s4[:, sl],
                     jnp.where(lane_p < 192, s8[:, sl], s16[:, sl])))
    pos = ((s - 1) * c + 1 + lax.broadcasted_iota(jnp.int32, (1, c, POOL_DIM), 1)).astype(F32)
    win = jnp.where(lane_p < 64, 2.0, jnp.where(lane_p < 128, 4.0, jnp.where(lane_p < 192, 8.0, 16.0)))
    pooled = sums * (1.0 / jnp.minimum(pos, win)) - hp
    y = jnp.dot(pooled.reshape(nb * c, POOL_DIM).astype(BF16), pw_ref[...], preferred_element_type=F32)
    mixed_buf[:, :, A_DIM:A_DIM + POOL_DIM] = y.reshape(nb, c, POOL_DIM)
    pool_hist[...] = pext[:, c:c + POOL_HIST, :]
    filler(1)

    mext = jnp.concatenate([m_hist[...], rest_buf[:, :, CC_OFF:CC_OFF + CONV_DIM]
                            * rest_buf[:, :, CH_OFF:CH_OFF + CONV_DIM]], axis=1)
    yc = (mext * sw_ref[2:3, :] + pltpu.roll(mext, 1, 1) * sw_ref[1:2, :]
          + pltpu.roll(mext, 2, 1) * sw_ref[0:1, :])
    mixed_buf[:, :, A_DIM + POOL_DIM:D_MODEL] = (rest_buf[:, :, CB_OFF:CB_OFF + CONV_DIM]
                                                 * yc[:, QKV_HIST:QKV_HIST + c, :])
    m_hist[...] = mext[:, c:c + QKV_HIST, :]
    filler(3)

    ab = rest_buf[:, :, AB_OFF:AB_OFF + AB_PAD]
    filler(1)
    pairs = nb // 2
    abt = jnp.swapaxes(ab.reshape(pairs, 2 * c, AB_PAD), 1, 2)[:, 0:2 * HEADS, :]
    row_t = lax.broadcasted_iota(jnp.int32, (1, 2 * HEADS, 2 * c), 1)
    lane_t = lax.broadcasted_iota(jnp.int32, (1, 2 * HEADS, 2 * c), 2)
    tpos = lane_t % c
    alog = jnp.zeros((1, 2 * HEADS, 2 * c), F32)
    dtb = jnp.zeros((1, 2 * HEADS, 2 * c), F32)
    for hd in range(HEADS):
        alog = jnp.where(row_t == hd, alog_ref[layer, hd], alog)
        dtb = jnp.where(row_t == hd, dtb_ref[layer, hd], dtb)
    gc_t = -jnp.exp(alog) * _softplus(abt + dtb)
    beta_t = _sigmoid(abt)
    sh = 1
    while sh < c:
        gc_t = gc_t + jnp.where(tpos >= sh, pltpu.roll(gc_t, sh, 2), 0.0)
        sh *= 2
    g_last_t = jnp.where(lane_t < c, gc_t[:, :, c - 1:c], gc_t[:, :, 2 * c - 1:2 * c])
    cols_t = jnp.concatenate(
        [gc_t[:, 0:HEADS], beta_t[:, HEADS:2 * HEADS], jnp.exp(gc_t)[:, 0:HEADS],
         jnp.exp(g_last_t - gc_t)[:, 0:HEADS], jnp.zeros((pairs, AB_PAD - 4 * HEADS, 2 * c), F32)], axis=1)
    cols = jnp.swapaxes(cols_t, 1, 2).reshape(nb, c, AB_PAD)
    gct = jnp.stack([gc_t[:, :, 0:c], gc_t[:, :, c:2 * c]], axis=1).reshape(nb, 2 * HEADS, c)

    out_gate = on_ref[...] * _silu_of_double(
        heads(lambda hd: rest_buf[:, :, Z_OFF + hd * HEAD_DIM:Z_OFF + (hd + 1) * HEAD_DIM]))
    filler(2)
    o, new_state = _delta_chunk(
        qr, kr, v,
        heads(lambda hd: cols[:, :, hd:hd + 1]),
        heads(lambda hd: gct[:, hd:hd + 1, :]),
        heads(lambda hd: cols[:, :, HEADS + hd:HEADS + hd + 1]),
        heads(lambda hd: cols[:, :, 2 * HEADS + hd:2 * HEADS + hd + 1]),
        heads(lambda hd: cols[:, :, 3 * HEADS + hd:3 * HEADS + hd + 1]),
        state_ref[...], filler)
    state_ref[...] = new_state
    o = o * lax.rsqrt(jnp.mean(o * o, axis=-1, keepdims=True) + EPS) * out_gate
    for hd in range(HEADS):
        mixed_buf[:, :, hd * HEAD_DIM:(hd + 1) * HEAD_DIM] = o[hd * nb:(hd + 1) * nb]
    filler(D_IN_PAD // A_TILE + 1)

    h_buf[...] = _rms_scale(xn_ref[...].reshape(nb * c, D_MODEL)).astype(BF16)

    mixed = mixed_buf[...].reshape(nb * c, D_MODEL).astype(BF16)
    out = xc_ref[...].reshape(nb * c, D_MODEL) + jnp.dot(mixed, wout_ref[...], preferred_element_type=F32)
    o_ref[...] = out.reshape(nb, c, D_MODEL)


def _ffn_kernel(x_ref, p_ref, n2_ref, wg_ref, wu_ref, wd_ref, pg_ref, pp_ref, fg_ref, o_ref, ff_buf,
                *, final_norm):
    x = x_ref[...]
    h = _rms_norm(x, n2_ref[...]).astype(BF16)
    for ci in range(D_FF // FF_CHUNK):
        sl = slice(ci * FF_CHUNK, (ci + 1) * FF_CHUNK)
        gate = jnp.dot(h, wg_ref[:, sl], preferred_element_type=F32)
        up = jnp.dot(h, wu_ref[:, sl], preferred_element_type=F32)
        ff_buf[:, sl] = (_silu(gate) * up).astype(BF16)
    x = x + jnp.dot(ff_buf[...], wd_ref[...], preferred_element_type=F32)
    gate = _sigmoid(jnp.dot(x.astype(BF16), pg_ref[...], preferred_element_type=F32))
    emb = jnp.dot(p_ref[...].astype(BF16), pp_ref[...], preferred_element_type=F32)
    x = x + gate * emb
    if final_norm:
        x = _rms_norm(x, fg_ref[...])
    o_ref[...] = x


def _prep_win_kernel(win_ref, g_ref, win_o):
    g = g_ref[...]
    win_o[0:QKV_DIM, :] = (win_ref[0:QKV_DIM, :] * g).astype(BF16)
    win_o[QKV_DIM:AB_LO, :] = (win_ref[QKV_DIM:AB_LO, :] * (0.5 * g)).astype(BF16)
    win_o[AB_LO:D_IN_PAD - AB_PAD, :] = (win_ref[AB_LO + 2 * HEADS:D_IN, :] * g).astype(BF16)
    win_o[D_IN_PAD - AB_PAD:D_IN_PAD, :] = jnp.concatenate(
        [win_ref[AB_LO:AB_LO + 2 * HEADS, :] * g, jnp.zeros((AB_PAD - 2 * HEADS, D_MODEL), F32)],
        axis=0).astype(BF16)


def _prep_kernel(wout_ref, pw_ref, ps_ref, wout_o, pw_o):
    wout_o[...] = wout_ref[...].astype(BF16)
    r = lax.broadcasted_iota(jnp.int32, (POOL_GROUP_DIM, POOL_DIM), 0)
    l = lax.broadcasted_iota(jnp.int32, (POOL_GROUP_DIM, POOL_DIM), 1)
    spread = jnp.where(l % POOL_GROUP_DIM == r, 1.0, 0.0).astype(BF16)
    tiled = jnp.dot(pw_ref[...].astype(BF16), spread, preferred_element_type=F32)
    rr = lax.broadcasted_iota(jnp.int32, (POOL_DIM, POOL_DIM), 0)
    ll = lax.broadcasted_iota(jnp.int32, (POOL_DIM, POOL_DIM), 1)
    pw_o[...] = (jnp.where(rr // POOL_GROUP_DIM == ll // POOL_GROUP_DIM, tiled, 0.0) * ps_ref[...]).astype(BF16)


def _prep_call(w_in, norm1_g, w_out, pool_w, pool_scale):
    depth = w_in.shape[0]
    params = pltpu.CompilerParams(dimension_semantics=("arbitrary", "arbitrary"),
                                  vmem_limit_bytes=VMEM_LIMIT_BYTES)
    w_in_t = jnp.swapaxes(w_in, 1, 2)
    win = pl.pallas_call(
        _prep_win_kernel,
        grid=(depth, 1),
        in_specs=[pl.BlockSpec((None, D_IN, D_MODEL), lambda l, r: (l, 0, 0)),
                  pl.BlockSpec((None, 1, D_MODEL), lambda l, r: (l, 0, 0))],
        out_specs=pl.BlockSpec((None, D_IN_PAD, D_MODEL), lambda l, r: (l, 0, 0)),
        out_shape=jax.ShapeDtypeStruct((depth, D_IN_PAD, D_MODEL), BF16),
        compiler_params=params,
        name="prep_win",
    )(w_in_t, norm1_g)

    pool_w = pool_w.reshape(depth, POOL_DIM, POOL_GROUP_DIM)
    wout, pw = pl.pallas_call(
        _prep_kernel,
        grid=(depth, 1),
        in_specs=[pl.BlockSpec((None, D_MODEL, D_MODEL), lambda l, r: (l, 0, 0)),
                  pl.BlockSpec((None, POOL_DIM, POOL_GROUP_DIM), lambda l, r: (l, 0, 0)),
                  pl.BlockSpec((None, 1, POOL_DIM), lambda l, r: (l, 0, 0))],
        out_specs=[pl.BlockSpec((None, D_MODEL, D_MODEL), lambda l, r: (l, 0, 0)),
                   pl.BlockSpec((None, POOL_DIM, POOL_DIM), lambda l, r: (l, 0, 0))],
        out_shape=[jax.ShapeDtypeStruct((depth, D_MODEL, D_MODEL), BF16),
                   jax.ShapeDtypeStruct((depth, POOL_DIM, POOL_DIM), BF16)],
        compiler_params=params,
        name="prep",
    )(w_out, pool_w, pool_scale)
    return win, wout, pw


def _layer_spec(a, layer):
    nd = a.ndim - 1
    return pl.BlockSpec((None,) + a.shape[1:], lambda *_: (layer,) + (0,) * nd,
                        pipeline_mode=pl.Buffered(1))


_SMEM_SPEC = pl.BlockSpec(memory_space=pltpu.SMEM)


def _cast_spec(a, layer, n_blocks, axis):
    shape = [None, a.shape[1], a.shape[2]]
    shape[axis] = a.shape[axis] // n_blocks

    def index(s):
        idx = [layer, 0, 0]
        idx[axis] = jnp.minimum(s, n_blocks - 1)
        return tuple(idx)

    return pl.BlockSpec(tuple(shape), index)


def _mixer_call(x, layer, win, cw, alog, dtb, on, pw, sw, wout, ffn_weights):
    nb, seq, _ = x.shape
    c = CHUNK
    n_chunks = seq // c
    nxt = pl.BlockSpec((nb, c, D_MODEL), lambda s: (0, jnp.minimum(s + 1, n_chunks - 1), 0))
    cur = pl.BlockSpec((nb, c, D_MODEL), lambda s: (0, jnp.maximum(s - 1, 0), 0))
    vmem_consts = (win, cw, on, pw, sw, wout)
    w_gate, w_up, w_down, ple_gate, ple_proj = ffn_weights
    cast_plan = ((w_gate, 32, 1), (w_up, 32, 1), (w_down, 8, 2), (ple_gate, 32, 1), (ple_proj, 16, 1))
    cast_specs = [_cast_spec(a, layer, n, ax) for a, n, ax in cast_plan]
    cast_specs_out = [_cast_spec(a, 0, n, ax) for a, n, ax in cast_plan]
    outs = pl.pallas_call(
        functools.partial(_mixer_kernel, layer=layer),
        grid=(n_chunks + 1,),
        in_specs=[nxt, cur, _SMEM_SPEC, _SMEM_SPEC] + [_layer_spec(a, layer) for a in vmem_consts]
                 + cast_specs,
        out_specs=[cur] + cast_specs_out,
        out_shape=[jax.ShapeDtypeStruct(x.shape, F32)]
                  + [jax.ShapeDtypeStruct((1,) + a.shape[1:], BF16) for a, _, _ in cast_plan],
        scratch_shapes=[
            pltpu.VMEM((nb * c, D_MODEL), BF16),
            pltpu.VMEM((nb, QKV_HIST, QKV_DIM), F32),
            pltpu.VMEM((nb, c, QKV_DIM), F32),
            pltpu.VMEM((nb, c, REST_DIM), F32),
            pltpu.VMEM((nb, POOL_HIST, POOL_DIM), F32),
            pltpu.VMEM((nb, QKV_HIST, CONV_DIM), F32),
            pltpu.VMEM((nb, c, D_MODEL), F32),
            pltpu.VMEM((HEADS * nb, HEAD_DIM, HEAD_DIM), F32),
        ],
        compiler_params=pltpu.CompilerParams(
            dimension_semantics=("arbitrary",), vmem_limit_bytes=VMEM_LIMIT_BYTES),
        name="mixer",
    )(x, x, alog, dtb, *vmem_consts, w_gate, w_up, w_down, ple_gate, ple_proj)
    return outs[0], outs[1:]


def _ffn_call(x2d, p3d, layer, n2, ffn_weights, fg, final_norm):
    tokens = x2d.shape[0]
    return pl.pallas_call(
        functools.partial(_ffn_kernel, final_norm=final_norm),
        grid=(tokens // FFN_TILE,),
        in_specs=[pl.BlockSpec((FFN_TILE, D_MODEL), lambda i: (i, 0)),
                  pl.BlockSpec((None, FFN_TILE, PLE_DIM), lambda i: (layer, i, 0))]
                 + [_layer_spec(n2, layer)] + [_layer_spec(a, 0) for a in ffn_weights]
                 + [pl.BlockSpec(fg.shape, lambda i: (0, 0), pipeline_mode=pl.Buffered(1))],
        out_specs=pl.BlockSpec((FFN_TILE, D_MODEL), lambda i: (i, 0)),
        out_shape=jax.ShapeDtypeStruct(x2d.shape, F32),
        scratch_shapes=[pltpu.VMEM((FFN_TILE, D_FF), BF16)],
        compiler_params=pltpu.CompilerParams(
            dimension_semantics=("arbitrary",), vmem_limit_bytes=VMEM_LIMIT_BYTES),
        name="ffn",
    )(x2d, p3d, n2, *ffn_weights, fg)


def kernel(x, p, norm1_g, w_in, conv_qkv, a_log, dt_bias, onorm_g, pool_w, pool_scale, sconv_w, w_out,
           norm2_g, w_gate, w_up, w_down, ple_proj, ple_gate, final_g):
    depth = w_in.shape[0]
    nb, seq, _ = x.shape
    n1, on, ps, n2 = (a.reshape(depth, 1, a.shape[-1]) for a in (norm1_g, onorm_g, pool_scale, norm2_g))
    win, wout, pw = _prep_call(w_in, n1, w_out, pool_w, ps)
    p3d = p.reshape(depth, nb * seq, PLE_DIM)
    fg = final_g.reshape(1, D_MODEL)
    for i in range(depth):
        x, ffn_weights = _mixer_call(x, i, win, conv_qkv, a_log, dt_bias, on, pw, sconv_w, wout,
                                     (w_gate, w_up, w_down, ple_gate, ple_proj))
        x = _ffn_call(x.reshape(nb * seq, D_MODEL), p3d, i, n2, ffn_weights, fg,
                      final_norm=(i == depth - 1)).reshape(nb, seq, D_MODEL)
    return x
```

```python
import functools

import jax
import jax.numpy as jnp
from jax import lax
from jax.experimental import pallas as pl
from jax.experimental.pallas import tpu as pltpu

D_MODEL = 1024
PLE_DIM = 256
EPS = 1e-6
HEAD_DIM = 128
HEADS = 4
A_DIM = HEADS * HEAD_DIM
QKV_DIM = 3 * A_DIM
QKV_CONV_WIDTH = 4
CHUNK = 64
POOL_WINDOWS = (2, 4, 8, 16)
POOL_DIM = 256
POOL_GROUP_DIM = 64
CONV_DIM = 256
CONV_WIDTH = 3
D_FF = 2816
AB_PAD = 128
REST_DIM = A_DIM + POOL_DIM + 3 * CONV_DIM + AB_PAD
Z_OFF, POOL_OFF, CB_OFF, CC_OFF, CH_OFF, AB_OFF = 0, 512, 768, 1024, 1280, 1536
D_IN_PAD = QKV_DIM + REST_DIM
AB_LO = 4 * A_DIM
D_IN = AB_LO + 2 * HEADS + POOL_DIM + 3 * CONV_DIM

QKV_HIST = 8
POOL_HIST = 16
VMEM_LIMIT_BYTES = 56 * 1024 * 1024
FFN_TILE = 1024
FF_CHUNK = 256
A_TILE = 256

BF16 = jnp.bfloat16
F32 = jnp.float32


def _sigmoid(x):
    return 0.5 * jnp.tanh(0.5 * x) + 0.5


def _silu(x):
    hx = 0.5 * x
    return hx + hx * jnp.tanh(hx)


def _silu_of_double(hx):
    return hx + hx * jnp.tanh(hx)


def _softplus(x):
    return jnp.maximum(x, 0.0) + jnp.log1p(jnp.exp(-jnp.abs(x)))


def _rms_scale(x):
    return x * lax.rsqrt(jnp.mean(x * x, axis=-1, keepdims=True) + EPS)


def _rms_norm(x, g):
    return _rms_scale(x) * g


def _bmm(a, b):
    return lax.dot_general(a.astype(BF16), b.astype(BF16), (((2,), (1,)), ((0,), (0,))),
                           preferred_element_type=F32)


def _bmm_nt(a, b):
    return lax.dot_general(a.astype(BF16), b.astype(BF16), (((2,), (2,)), ((0,), (0,))),
                           preferred_element_type=F32)


def _delta_chunk(qr, kr, v, gcol, grow, beta, eg, kd_scale, state, filler):
    c = CHUNK
    rq = lax.rsqrt(jnp.sum(qr * qr, axis=-1, keepdims=True) + EPS) * (HEAD_DIM ** -0.5)
    rk = lax.rsqrt(jnp.sum(kr * kr, axis=-1, keepdims=True) + EPS)
    q = qr * rq
    k = kr * rk
    g_last = gcol[:, c - 1:c, :]
    kb = k * beta
    vb = v * beta
    row = lax.broadcasted_iota(jnp.int32, (1, c, c), 1)
    col = lax.broadcasted_iota(jnp.int32, (1, c, c), 2)
    decay = jnp.exp(jnp.where(row >= col, gcol - grow, -1e30))
    kq = _bmm_nt(jnp.concatenate([kb, q], axis=1), k)
    lower = jnp.where(row > col, kq[:, 0:c] * decay, 0.0)
    attn = kq[:, c:2 * c] * decay
    filler(1)
    x = -lower
    t = jnp.where(row == col, 1.0, 0.0) + x
    x = _bmm(x, x)
    n_sq = CHUNK.bit_length() - 2
    for _ in range(n_sq - 1):
        tx = _bmm(jnp.concatenate([x, t], axis=1), x)
        x = tx[:, 0:c]
        t = t + tx[:, c:2 * c]
        filler(1)
    t = t + _bmm(t, x)
    uw = _bmm(t, jnp.concatenate([vb, kb * eg], axis=2))
    ws = _bmm(jnp.concatenate([uw[:, :, HEAD_DIM:2 * HEAD_DIM], q * eg], axis=1), state)
    v_new = uw[:, :, 0:HEAD_DIM] - ws[:, 0:c]
    o = ws[:, c:2 * c] + _bmm(attn, v_new)
    kd = k * kd_scale
    new_state = state * jnp.exp(g_last) + _bmm(jnp.swapaxes(kd, 1, 2), v_new)
    return o, new_state


def _project_tile(h_buf, win_ref, qkv_ext, rest_buf, lo, nb):
    c = CHUNK
    hi = min(lo + A_TILE, D_IN_PAD)
    tile = lax.dot_general(h_buf[...], win_ref[lo:hi, :], (((1,), (1,)), ((), ())),
                           preferred_element_type=F32).reshape(nb, c, hi - lo)
    if hi <= QKV_DIM:
        qkv_ext[:, QKV_HIST:QKV_HIST + c, lo:hi] = tile
    else:
        rest_buf[:, :, lo - QKV_DIM:hi - QKV_DIM] = tile


def _mixer_kernel(xn_ref, xc_ref, alog_ref, dtb_ref, win_ref, cw_ref, on_ref, pw_ref,
                  sw_ref, wout_ref, wg_ref, wu_ref, wd_ref, pg_ref, pp_ref,
                  o_ref, wg_o, wu_o, wd_o, pg_o, pp_o,
                  h_buf, qkv_ext, rest_buf, pool_hist, m_hist, mixed_buf, state_ref, *, layer):
    s = pl.program_id(0)
    nb = xn_ref.shape[0]
    c = CHUNK

    for src_ref, dst_ref in ((wg_ref, wg_o), (wu_ref, wu_o), (wd_ref, wd_o), (pg_ref, pg_o), (pp_ref, pp_o)):
        dst_ref[...] = src_ref[...].astype(BF16)

    @pl.when(s == 0)
    def _():
        qkv_ext[:, 0:QKV_HIST, :] = jnp.zeros((nb, QKV_HIST, QKV_DIM), F32)
        pool_hist[...] = jnp.zeros(pool_hist.shape, F32)
        m_hist[...] = jnp.zeros(m_hist.shape, F32)
        state_ref[...] = jnp.zeros(state_ref.shape, F32)
        o_ref[...] = xc_ref[...]
        h_buf[...] = _rms_scale(xc_ref[...].reshape(nb * c, D_MODEL)).astype(BF16)
        for lo in range(0, D_IN_PAD, A_TILE):
            _project_tile(h_buf, win_ref, qkv_ext, rest_buf, lo, nb)
        h_buf[...] = _rms_scale(xn_ref[...].reshape(nb * c, D_MODEL)).astype(BF16)

    @pl.when(s > 0)
    def _():
        _mixer_step(s, xn_ref, xc_ref, alog_ref, dtb_ref, win_ref, cw_ref, on_ref, pw_ref,
                    sw_ref, wout_ref, o_ref,
                    h_buf, qkv_ext, rest_buf, pool_hist, m_hist, mixed_buf, state_ref, layer)


def _mixer_step(s, xn_ref, xc_ref, alog_ref, dtb_ref, win_ref, cw_ref, on_ref, pw_ref,
                sw_ref, wout_ref, o_ref,
                h_buf, qkv_ext, rest_buf, pool_hist, m_hist, mixed_buf, state_ref, layer):
    nb = xn_ref.shape[0]
    c = CHUNK

    rest_tiles = [QKV_DIM + off for off in (POOL_OFF, CB_OFF, CC_OFF, CH_OFF, AB_OFF, Z_OFF, Z_OFF + A_TILE)]
    a_tiles = iter(list(range(0, QKV_DIM, A_TILE)) + rest_tiles)

    def filler(n):
        for _ in range(n):
            lo = next(a_tiles, None)
            if lo is None:
                return
            _project_tile(h_buf, win_ref, qkv_ext, rest_buf, lo, nb)


    act_tiles = []
    for lo_c in range(0, QKV_DIM, A_TILE):
        cols = slice(lo_c, lo_c + A_TILE)
        ext = qkv_ext[:, :, cols]
        taps = 0.5 * cw_ref[:, cols]
        acc = ext * taps[QKV_CONV_WIDTH - 1:QKV_CONV_WIDTH]
        for j in range(QKV_CONV_WIDTH - 1):
            acc = acc + pltpu.roll(ext, QKV_CONV_WIDTH - 1 - j, 1) * taps[j:j + 1]
        qkv_ext[:, 0:QKV_HIST, cols] = ext[:, c:c + QKV_HIST, :]
        filler(1)
        act_tiles.append(_silu_of_double(acc[:, QKV_HIST:QKV_HIST + c, :]))
    act = jnp.concatenate(act_tiles, axis=2)

    hp = rest_buf[:, :, POOL_OFF:POOL_OFF + POOL_DIM]
    pext = jnp.concatenate([pool_hist[...], hp], axis=1)
    s2 = pext + pltpu.roll(pext, 1, 1)
    s4 = s2 + pltpu.roll(s2, 2, 1)
    s8 = s4 + pltpu.roll(s4, 4, 1)
    s16 = s8 + pltpu.roll(s8, 8, 1)
    sl = slice(POOL_HIST, POOL_HIST + c)
    lane_p = lax.broadcasted_iota(jnp.int32, (1, c, POOL_DIM), 2)
    sums = jnp.where(lane_p < 64, s2[:, sl], jnp.where(lane_p < 128, s4[:, sl],
                     jnp.where(lane_p < 192, s8[:, sl], s16[:, sl])))
    pos = ((s - 1) * c + 1 + lax.broadcasted_iota(jnp.int32, (1, c, POOL_DIM), 1)).astype(F32)
    win = jnp.where(lane_p < 64, 2.0, jnp.where(lane_p < 128, 4.0, jnp.where(lane_p < 192, 8.0, 16.0)))
    pooled = sums * (1.0 / jnp.minimum(pos, win)) - hp
    y = jnp.dot(pooled.reshape(nb * c, POOL_DIM).astype(BF16), pw_ref[...], preferred_element_type=F32)
    mixed_buf[:, :, A_DIM:A_DIM + POOL_DIM] = y.reshape(nb, c, POOL_DIM)
    pool_hist[...] = pext[:, c:c + POOL_HIST, :]
    filler(1)

    mext = jnp.concatenate([m_hist[...], rest_buf[:, :, CC_OFF:CC_OFF + CONV_DIM]
                            * rest_buf[:, :, CH_OFF:CH_OFF + CONV_DIM]], axis=1)
    yc = (mext * sw_ref[2:3, :] + pltpu.roll(mext, 1, 1) * sw_ref[1:2, :]
          + pltpu.roll(mext, 2, 1) * sw_ref[0:1, :])
    mixed_buf[:, :, A_DIM + POOL_DIM:D_MODEL] = (rest_buf[:, :, CB_OFF:CB_OFF + CONV_DIM]
                                                 * yc[:, QKV_HIST:QKV_HIST + c, :])
    m_hist[...] = mext[:, c:c + QKV_HIST, :]
    filler(3)

    ab = rest_buf[:, :, AB_OFF:AB_OFF + AB_PAD]
    filler(1)
    pairs = nb // 2
    abt = jnp.swapaxes(ab.reshape(pairs, 2 * c, AB_PAD), 1, 2)[:, 0:2 * HEADS, :]
    row_t = lax.broadcasted_iota(jnp.int32, (1, 2 * HEADS, 2 * c), 1)
    lane_t = lax.broadcasted_iota(jnp.int32, (1, 2 * HEADS, 2 * c), 2)
    tpos = lane_t % c
    alog = jnp.zeros((1, 2 * HEADS, 2 * c), F32)
    dtb = jnp.zeros((1, 2 * HEADS, 2 * c), F32)
    for hd in range(HEADS):
        alog = jnp.where(row_t == hd, alog_ref[layer, hd], alog)
        dtb = jnp.where(row_t == hd, dtb_ref[layer, hd], dtb)
    gc_t = -jnp.exp(alog) * _softplus(abt + dtb)
    beta_t = _sigmoid(abt)
    sh = 1
    while sh < c:
        gc_t = gc_t + jnp.where(tpos >= sh, pltpu.roll(gc_t, sh, 2), 0.0)
        sh *= 2
    g_last_t = jnp.where(lane_t < c, gc_t[:, :, c - 1:c], gc_t[:, :, 2 * c - 1:2 * c])
    cols_t = jnp.concatenate(
        [gc_t[:, 0:HEADS], beta_t[:, HEADS:2 * HEADS], jnp.exp(gc_t)[:, 0:HEADS],
         jnp.exp(g_last_t - gc_t)[:, 0:HEADS], jnp.zeros((pairs, AB_PAD - 4 * HEADS, 2 * c), F32)], axis=1)
    cols = jnp.swapaxes(cols_t, 1, 2).reshape(nb, c, AB_PAD)
    gct = jnp.stack([gc_t[:, :, 0:c], gc_t[:, :, c:2 * c]], axis=1).reshape(nb, 2 * HEADS, c)

    def heads(f):
        return jnp.concatenate([f(hd) for hd in range(HEADS)], axis=0)

    out_gate = on_ref[...] * _silu_of_double(
        heads(lambda hd: rest_buf[:, :, Z_OFF + hd * HEAD_DIM:Z_OFF + (hd + 1) * HEAD_DIM]))
    filler(2)
    o, new_state = _delta_chunk(
        heads(lambda hd: act[:, :, hd * HEAD_DIM:(hd + 1) * HEAD_DIM]),
        heads(lambda hd: act[:, :, A_DIM + hd * HEAD_DIM:A_DIM + (hd + 1) * HEAD_DIM]),
        heads(lambda hd: act[:, :, 2 * A_DIM + hd * HEAD_DIM:2 * A_DIM + (hd + 1) * HEAD_DIM]),
        heads(lambda hd: cols[:, :, hd:hd + 1]),
        heads(lambda hd: gct[:, hd:hd + 1, :]),
        heads(lambda hd: cols[:, :, HEADS + hd:HEADS + hd + 1]),
        heads(lambda hd: cols[:, :, 2 * HEADS + hd:2 * HEADS + hd + 1]),
        heads(lambda hd: cols[:, :, 3 * HEADS + hd:3 * HEADS + hd + 1]),
        state_ref[...], filler)
    state_ref[...] = new_state
    o = o * lax.rsqrt(jnp.mean(o * o, axis=-1, keepdims=True) + EPS) * out_gate
    for hd in range(HEADS):
        mixed_buf[:, :, hd * HEAD_DIM:(hd + 1) * HEAD_DIM] = o[hd * nb:(hd + 1) * nb]
    filler(D_IN_PAD // A_TILE + 1)

    h_buf[...] = _rms_scale(xn_ref[...].reshape(nb * c, D_MODEL)).astype(BF16)

    mixed = mixed_buf[...].reshape(nb * c, D_MODEL).astype(BF16)
    out = xc_ref[...].reshape(nb * c, D_MODEL) + jnp.dot(mixed, wout_ref[...], preferred_element_type=F32)
    o_ref[...] = out.reshape(nb, c, D_MODEL)


def _ffn_weight_copies(hbm_refs, vmem_refs, sem):
    (wg_h, wu_h, wd_h, pg_h, pp_h), (wg_v, wu_v, wd_v, pg_v, pp_v) = hbm_refs, vmem_refs
    pairs = []
    for ci in range(D_FF // FF_CHUNK):
        sl = slice(ci * FF_CHUNK, (ci + 1) * FF_CHUNK)
        pairs += [(wg_h.at[0, :, sl], wg_v.at[:, sl]), (wu_h.at[0, :, sl], wu_v.at[:, sl])]
    pairs += [(wd_h.at[0], wd_v), (pg_h.at[0], pg_v), (pp_h.at[0], pp_v)]
    return [pltpu.make_async_copy(src, dst, sem.at[k]) for k, (src, dst) in enumerate(pairs)]


def _ffn_kernel(x_ref, p_ref, n2_ref, wg_h, wu_h, wd_h, pg_h, pp_h, fg_ref, o_ref,
                ff_buf, wg_ref, wu_ref, wd_ref, pg_ref, pp_ref, sem, *, final_norm):
    copies = _ffn_weight_copies((wg_h, wu_h, wd_h, pg_h, pp_h), (wg_ref, wu_ref, wd_ref, pg_ref, pp_ref), sem)
    n_ff = D_FF // FF_CHUNK

    def body(arrived):
        x = x_ref[...]
        h = _rms_norm(x, n2_ref[...]).astype(BF16)
        for ci in range(n_ff):
            sl = slice(ci * FF_CHUNK, (ci + 1) * FF_CHUNK)
            arrived(2 * ci)
            gate = jnp.dot(h, wg_ref[:, sl], preferred_element_type=F32)
            arrived(2 * ci + 1)
            up = jnp.dot(h, wu_ref[:, sl], preferred_element_type=F32)
            ff_buf[:, sl] = (_silu(gate) * up).astype(BF16)
        arrived(2 * n_ff)
        x = x + jnp.dot(ff_buf[...], wd_ref[...], preferred_element_type=F32)
        arrived(2 * n_ff + 1)
        gate = _sigmoid(jnp.dot(x.astype(BF16), pg_ref[...], preferred_element_type=F32))
        arrived(2 * n_ff + 2)
        emb = jnp.dot(p_ref[...].astype(BF16), pp_ref[...], preferred_element_type=F32)
        x = x + gate * emb
        if final_norm:
            x = _rms_norm(x, fg_ref[...])
        o_ref[...] = x

    @pl.when(pl.program_id(0) == 0)
    def _():
        for cp in copies:
            cp.start()
        body(lambda k: copies[k].wait())

    @pl.when(pl.program_id(0) > 0)
    def _():
        body(lambda k: None)


def _prep_win_kernel(win_ref, g_ref, win_o):
    g = g_ref[...]
    win_o[0:QKV_DIM, :] = (win_ref[0:QKV_DIM, :] * g).astype(BF16)
    win_o[QKV_DIM:AB_LO, :] = (win_ref[QKV_DIM:AB_LO, :] * (0.5 * g)).astype(BF16)
    win_o[AB_LO:D_IN_PAD - AB_PAD, :] = (win_ref[AB_LO + 2 * HEADS:D_IN, :] * g).astype(BF16)
    win_o[D_IN_PAD - AB_PAD:D_IN_PAD, :] = jnp.concatenate(
        [win_ref[AB_LO:AB_LO + 2 * HEADS, :] * g, jnp.zeros((AB_PAD - 2 * HEADS, D_MODEL), F32)],
        axis=0).astype(BF16)


def _prep_kernel(wout_ref, pw_ref, ps_ref, wout_o, pw_o):
    wout_o[...] = wout_ref[...].astype(BF16)
    r = lax.broadcasted_iota(jnp.int32, (POOL_GROUP_DIM, POOL_DIM), 0)
    l = lax.broadcasted_iota(jnp.int32, (POOL_GROUP_DIM, POOL_DIM), 1)
    spread = jnp.where(l % POOL_GROUP_DIM == r, 1.0, 0.0).astype(BF16)
    tiled = jnp.dot(pw_ref[...].astype(BF16), spread, preferred_element_type=F32)
    rr = lax.broadcasted_iota(jnp.int32, (POOL_DIM, POOL_DIM), 0)
    ll = lax.broadcasted_iota(jnp.int32, (POOL_DIM, POOL_DIM), 1)
    pw_o[...] = (jnp.where(rr // POOL_GROUP_DIM == ll // POOL_GROUP_DIM, tiled, 0.0) * ps_ref[...]).astype(BF16)


def _prep_call(w_in, norm1_g, w_out, pool_w, pool_scale):
    depth = w_in.shape[0]
    params = pltpu.CompilerParams(dimension_semantics=("arbitrary", "arbitrary"),
                                  vmem_limit_bytes=VMEM_LIMIT_BYTES)
    w_in_t = jnp.swapaxes(w_in, 1, 2)
    win = pl.pallas_call(
        _prep_win_kernel,
        grid=(depth, 1),
        in_specs=[pl.BlockSpec((None, D_IN, D_MODEL), lambda l, r: (l, 0, 0)),
                  pl.BlockSpec((None, 1, D_MODEL), lambda l, r: (l, 0, 0))],
        out_specs=pl.BlockSpec((None, D_IN_PAD, D_MODEL), lambda l, r: (l, 0, 0)),
        out_shape=jax.ShapeDtypeStruct((depth, D_IN_PAD, D_MODEL), BF16),
        compiler_params=params,
        name="prep_win",
    )(w_in_t, norm1_g)

    pool_w = pool_w.reshape(depth, POOL_DIM, POOL_GROUP_DIM)
    wout, pw = pl.pallas_call(
        _prep_kernel,
        grid=(depth, 1),
        in_specs=[pl.BlockSpec((None, D_MODEL, D_MODEL), lambda l, r: (l, 0, 0)),
                  pl.BlockSpec((None, POOL_DIM, POOL_GROUP_DIM), lambda l, r: (l, 0, 0)),
                  pl.BlockSpec((None, 1, POOL_DIM), lambda l, r: (l, 0, 0))],
        out_specs=[pl.BlockSpec((None, D_MODEL, D_MODEL), lambda l, r: (l, 0, 0)),
                   pl.BlockSpec((None, POOL_DIM, POOL_DIM), lambda l, r: (l, 0, 0))],
        out_shape=[jax.ShapeDtypeStruct((depth, D_MODEL, D_MODEL), BF16),
                   jax.ShapeDtypeStruct((depth, POOL_DIM, POOL_DIM), BF16)],
        compiler_params=params,
        name="prep",
    )(w_out, pool_w, pool_scale)
    return win, wout, pw


def _layer_spec(a, layer):
    nd = a.ndim - 1
    return pl.BlockSpec((None,) + a.shape[1:], lambda *_: (layer,) + (0,) * nd,
                        pipeline_mode=pl.Buffered(1))


_SMEM_SPEC = pl.BlockSpec(memory_space=pltpu.SMEM)


def _cast_spec(a, layer, n_blocks, axis):
    shape = [None, a.shape[1], a.shape[2]]
    shape[axis] = a.shape[axis] // n_blocks

    def index(s):
        idx = [layer, 0, 0]
        idx[axis] = jnp.minimum(s, n_blocks - 1)
        return tuple(idx)

    return pl.BlockSpec(tuple(shape), index)


def _mixer_call(x, layer, win, cw, alog, dtb, on, pw, sw, wout, ffn_weights):
    nb, seq, _ = x.shape
    c = CHUNK
    n_chunks = seq // c
    nxt = pl.BlockSpec((nb, c, D_MODEL), lambda s: (0, jnp.minimum(s + 1, n_chunks - 1), 0))
    cur = pl.BlockSpec((nb, c, D_MODEL), lambda s: (0, jnp.maximum(s - 1, 0), 0))
    vmem_consts = (win, cw, on, pw, sw, wout)
    w_gate, w_up, w_down, ple_gate, ple_proj = ffn_weights
    cast_plan = ((w_gate, 32, 1), (w_up, 32, 1), (w_down, 8, 2), (ple_gate, 32, 1), (ple_proj, 16, 1))
    cast_specs = [_cast_spec(a, layer, n, ax) for a, n, ax in cast_plan]
    cast_specs_out = [_cast_spec(a, 0, n, ax) for a, n, ax in cast_plan]
    outs = pl.pallas_call(
        functools.partial(_mixer_kernel, layer=layer),
        grid=(n_chunks + 1,),
        in_specs=[nxt, cur, _SMEM_SPEC, _SMEM_SPEC] + [_layer_spec(a, layer) for a in vmem_consts]
                 + cast_specs,
        out_specs=[cur] + cast_specs_out,
        out_shape=[jax.ShapeDtypeStruct(x.shape, F32)]
                  + [jax.ShapeDtypeStruct((1,) + a.shape[1:], BF16) for a, _, _ in cast_plan],
        scratch_shapes=[
            pltpu.VMEM((nb * c, D_MODEL), BF16),
            pltpu.VMEM((nb, QKV_HIST + c, QKV_DIM), F32),
            pltpu.VMEM((nb, c, REST_DIM), F32),
            pltpu.VMEM((nb, POOL_HIST, POOL_DIM), F32),
            pltpu.VMEM((nb, QKV_HIST, CONV_DIM), F32),
            pltpu.VMEM((nb, c, D_MODEL), F32),
            pltpu.VMEM((HEADS * nb, HEAD_DIM, HEAD_DIM), F32),
        ],
        compiler_params=pltpu.CompilerParams(
            dimension_semantics=("arbitrary",), vmem_limit_bytes=VMEM_LIMIT_BYTES),
        name="mixer",
    )(x, x, alog, dtb, *vmem_consts, w_gate, w_up, w_down, ple_gate, ple_proj)
    return outs[0], outs[1:]


def _ffn_call(x2d, p3d, layer, n2, ffn_weights, fg, final_norm):
    tokens = x2d.shape[0]
    return pl.pallas_call(
        functools.partial(_ffn_kernel, final_norm=final_norm),
        grid=(tokens // FFN_TILE,),
        in_specs=[pl.BlockSpec((FFN_TILE, D_MODEL), lambda i: (i, 0)),
                  pl.BlockSpec((None, FFN_TILE, PLE_DIM), lambda i: (layer, i, 0))]
                 + [_layer_spec(n2, layer)] + [pl.BlockSpec(memory_space=pl.ANY) for _ in ffn_weights]
                 + [pl.BlockSpec(fg.shape, lambda i: (0, 0), pipeline_mode=pl.Buffered(1))],
        out_specs=pl.BlockSpec((FFN_TILE, D_MODEL), lambda i: (i, 0)),
        out_shape=jax.ShapeDtypeStruct(x2d.shape, F32),
        scratch_shapes=[pltpu.VMEM((FFN_TILE, D_FF), BF16)]
                       + [pltpu.VMEM(a.shape[1:], BF16) for a in ffn_weights]
                       + [pltpu.SemaphoreType.DMA((2 * (D_FF // FF_CHUNK) + 3,))],
        compiler_params=pltpu.CompilerParams(
            dimension_semantics=("arbitrary",), vmem_limit_bytes=VMEM_LIMIT_BYTES),
        name="ffn",
    )(x2d, p3d, n2, *ffn_weights, fg)


def kernel(x, p, norm1_g, w_in, conv_qkv, a_log, dt_bias, onorm_g, pool_w, pool_scale, sconv_w, w_out,
           norm2_g, w_gate, w_up, w_down, ple_proj, ple_gate, final_g):
    depth = w_in.shape[0]
    nb, seq, _ = x.shape
    n1, on, ps, n2 = (a.reshape(depth, 1, a.shape[-1]) for a in (norm1_g, onorm_g, pool_scale, norm2_g))
    win, wout, pw = _prep_call(w_in, n1, w_out, pool_w, ps)
    p3d = p.reshape(depth, nb * seq, PLE_DIM)
    fg = final_g.reshape(1, D_MODEL)
    for i in range(depth):
        x, ffn_weights = _mixer_call(x, i, win, conv_qkv, a_log, dt_bias, on, pw, sconv_w, wout,
                                     (w_gate, w_up, w_down, ple_gate, ple_proj))
        x = _ffn_call(x.reshape(nb * seq, D_MODEL), p3d, i, n2, ffn_weights, fg,
                      final_norm=(i == depth - 1)).reshape(nb, seq, D_MODEL)
    return x
```
